```python
import math
import jax, jax.numpy as jnp
from jax import lax
import numpy as np

D_MODEL = 2048
BATCH = 4
SEQ = 8192
DEPTH = 4

CONV_WIDTH = D_MODEL // 2
CONV_KERNEL = 31
N_HEADS = 8
HEAD_DIM = D_MODEL // (4 * N_HEADS)
V_DIM = 2 * HEAD_DIM
ATTN_WIDTH = N_HEADS * V_DIM
QK_WIDTH = 2 * N_HEADS * HEAD_DIM
Q_BLOCK = 128
NORM_EPS = 1e-6
SPLITS = (CONV_WIDTH, CONV_WIDTH, CONV_WIDTH, QK_WIDTH, QK_WIDTH, ATTN_WIDTH, ATTN_WIDTH, D_MODEL, D_MODEL)
IN_WIDTH = 3 * CONV_WIDTH + 2 * QK_WIDTH + 2 * ATTN_WIDTH + 2 * D_MODEL

kernel_name = "hybrid_conformer_conv_diff_attn_gated_merge"


def rms_norm(x, g):
    xf = x.astype(jnp.float32)
    y = xf * lax.rsqrt(jnp.mean(xf * xf, axis=-1, keepdims=True) + NORM_EPS)
    return (y * g.astype(jnp.float32)).astype(x.dtype)


def layer_norm(x, g, b):
    xf = x.astype(jnp.float32)
    mu = jnp.mean(xf, axis=-1, keepdims=True)
    var = jnp.mean(jnp.square(xf - mu), axis=-1, keepdims=True)
    y = (xf - mu) * lax.rsqrt(var + NORM_EPS)
    return (y * g.astype(jnp.float32) + b.astype(jnp.float32)).astype(x.dtype)


def alibi_slopes():
    return 2.0 ** (-8.0 * jnp.arange(1, N_HEADS + 1, dtype=jnp.float32) / N_HEADS)


def lambda_init(layer_idx):
    return 0.8 - 0.6 * math.exp(-0.3 * layer_idx)


def conv_branch(cv, cg, cz, conv_w, conv_b, cn_g, cn_b, w_conv_out):
    u = cv * jax.nn.sigmoid(cg)
    k = conv_w.reshape(CONV_KERNEL, 1, CONV_WIDTH).astype(u.dtype)
    u = lax.conv_general_dilated(u, k, window_strides=(1,), padding=[(CONV_KERNEL - 1, 0)],
                                 dimension_numbers=("NWC", "WIO", "NWC"),
                                 feature_group_count=CONV_WIDTH)
    u = u + conv_b
    u = layer_norm(u, cn_g, cn_b)
    u = jax.nn.silu(u) * jax.nn.silu(cz)
    return u @ w_conv_out


def diff_attention(q1, q2, k1, k2, v, lam):
    B, H, S, _ = q1.shape
    nblk = S // Q_BLOCK
    scale = HEAD_DIM ** -0.5
    slopes = alibi_slopes()
    kpos = jnp.arange(S)
    qb1 = q1.reshape(B, H, nblk, Q_BLOCK, HEAD_DIM).transpose(2, 0, 1, 3, 4)
    qb2 = q2.reshape(B, H, nblk, Q_BLOCK, HEAD_DIM).transpose(2, 0, 1, 3, 4)

    def one_block(args):
        i, a1, a2 = args
        qpos = i * Q_BLOCK + jnp.arange(Q_BLOCK)
        dist = (qpos[:, None] - kpos[None, :]).astype(jnp.float32)
        bias = jnp.where(dist[None] >= 0, -slopes[:, None, None] * dist[None], -jnp.inf)
        s1 = jnp.einsum("bhqd,bhkd->bhqk", a1, k1).astype(jnp.float32) * scale + bias
        s2 = jnp.einsum("bhqd,bhkd->bhqk", a2, k2).astype(jnp.float32) * scale + bias
        attn = jax.nn.softmax(s1, axis=-1) - lam * jax.nn.softmax(s2, axis=-1)
        return jnp.einsum("bhqk,bhke->bhqe", attn.astype(v.dtype), v)

    out = lax.map(one_block, (jnp.arange(nblk), qb1, qb2))
    return out.transpose(1, 0, 3, 2, 4).reshape(B, S, H, V_DIM)


def attn_branch(q, k, v, az, layer_idx, lam_q1, lam_k1, lam_q2, lam_k2, subln_g, w_attn_out):
    B, S, _ = q.shape
    q = q.reshape(B, S, N_HEADS, 2, HEAD_DIM).transpose(0, 2, 3, 1, 4)
    k = k.reshape(B, S, N_HEADS, 2, HEAD_DIM).transpose(0, 2, 3, 1, 4)
    v = v.reshape(B, S, N_HEADS, V_DIM).transpose(0, 2, 1, 3)
    lam_0 = lambda_init(layer_idx)
    lam = (jnp.exp(jnp.sum(lam_q1.astype(jnp.float32) * lam_k1.astype(jnp.float32)))
           - jnp.exp(jnp.sum(lam_q2.astype(jnp.float32) * lam_k2.astype(jnp.float32))) + lam_0)
    o = diff_attention(q[:, :, 0], q[:, :, 1], k[:, :, 0], k[:, :, 1], v, lam)
    o = rms_norm(o, subln_g) * (1.0 - lam_0)
    o = o.reshape(B, S, ATTN_WIDTH) * jax.nn.silu(az)
    return o @ w_attn_out


def setup_inputs(seed: int = 0) -> dict:
    key = jax.random.key(seed)
    ks = jax.random.split(key, 24)
    f32 = jnp.float32
    nrm = lambda k, shape, s: jax.random.normal(k, shape, f32) * s
    return {
        "x": nrm(ks[0], (BATCH, SEQ, D_MODEL), 1.0),
        "c": nrm(ks[1], (BATCH, D_MODEL), 1.0),
        "w_ada": nrm(ks[2], (DEPTH, D_MODEL, 3 * D_MODEL), D_MODEL ** -0.5),
        "b_ada": nrm(ks[3], (DEPTH, 3 * D_MODEL), 0.01),
        "g_pre": 1.0 + nrm(ks[4], (DEPTH, D_MODEL), 0.02),
        "g_post": 1.0 + nrm(ks[5], (DEPTH, D_MODEL), 0.02),
        "w_in": nrm(ks[6], (DEPTH, D_MODEL, IN_WIDTH), D_MODEL ** -0.5),
        "conv_w": nrm(ks[7], (DEPTH, CONV_KERNEL, CONV_WIDTH), CONV_KERNEL ** -0.5),
        "conv_b": nrm(ks[8], (DEPTH, CONV_WIDTH), 0.01),
        "cn_g": 1.0 + nrm(ks[9], (DEPTH, CONV_WIDTH), 0.02),
        "cn_b": nrm(ks[10], (DEPTH, CONV_WIDTH), 0.01),
        "w_conv_out": nrm(ks[11], (DEPTH, CONV_WIDTH, D_MODEL), CONV_WIDTH ** -0.5),
        "lam_q1": nrm(ks[12], (DEPTH, HEAD_DIM), 0.1),
        "lam_k1": nrm(ks[13], (DEPTH, HEAD_DIM), 0.1),
        "lam_q2": nrm(ks[14], (DEPTH, HEAD_DIM), 0.1),
        "lam_k2": nrm(ks[15], (DEPTH, HEAD_DIM), 0.1),
        "subln_g": 1.0 + nrm(ks[16], (DEPTH, V_DIM), 0.02),
        "w_attn_out": nrm(ks[17], (DEPTH, ATTN_WIDTH, D_MODEL), ATTN_WIDTH ** -0.5),
        "w_o": nrm(ks[18], (DEPTH, D_MODEL, D_MODEL), D_MODEL ** -0.5),
    }


def reference(x, c, w_ada, b_ada, g_pre, g_post, w_in, conv_w, conv_b, cn_g, cn_b, w_conv_out,
              lam_q1, lam_k1, lam_q2, lam_k2, subln_g, w_attn_out, w_o):
    offsets = np.cumsum(np.array(SPLITS))[:-1].tolist()
    c_act = jax.nn.silu(c)
    for l in range(DEPTH):
        mod = c_act @ w_ada[l] + b_ada[l]
        shift, scale, gate = jnp.split(mod, 3, axis=-1)
        h = rms_norm(x, g_pre[l]) * (1.0 + scale[:, None, :]) + shift[:, None, :]
        p = h @ w_in[l]
        cv, cg, cz, q, k, v, az, m_conv, m_attn = jnp.split(p, offsets, axis=-1)
        y_conv = conv_branch(cv, cg, cz, conv_w[l], conv_b[l], cn_g[l], cn_b[l], w_conv_out[l])
        y_attn = attn_branch(q, k, v, az, l, lam_q1[l], lam_k1[l], lam_q2[l], lam_k2[l],
                             subln_g[l], w_attn_out[l])
        merged = jax.nn.sigmoid(m_conv) * y_conv + jax.nn.sigmoid(m_attn) * y_attn
        out = merged @ w_o[l]
        x = x + gate[:, None, :] * rms_norm(out, g_post[l])
    return x
```

```python
import functools
import math

import jax
import jax.numpy as jnp
import numpy as np
from jax import lax
from jax.experimental import pallas as pl
from jax.experimental.pallas import tpu as pltpu

F32 = jnp.float32
BF16 = jnp.bfloat16

NORM_EPS = 1e-6
CONV_KERNEL = 31
N_HEADS = 8
LOG2E = math.log2(math.e)

LANES = 128
V7X_VMEM_BYTES = 64 * 1024 * 1024
VMEM_LIMIT = V7X_VMEM_BYTES - 8 * 1024 * 1024

COL_CV, COL_CG, COL_CZ, COL_Q, COL_K, COL_V, COL_AZ, COL_MC, COL_MA = 0, 1, 2, 3, 4, 5, 6, 7, 9


def _lambda_init(layer_idx):
    return 0.8 - 0.6 * math.exp(-0.3 * layer_idx)


def _sigmoid(v):
    return 1.0 / (1.0 + jnp.exp(-v))


def _silu(v):
    return v * _sigmoid(v)


def _ada_kernel(c_ref, w_ref, b_ref, o_ref):
    c = c_ref[...]
    ca = _silu(c).astype(BF16)
    o_ref[0] = jnp.dot(ca, w_ref[0].astype(BF16), preferred_element_type=F32) + b_ref[0]


def _ada_call(c, w_ada, b_ada):
    depth, d, d3 = w_ada.shape
    bsz = c.shape[0]
    bn = 1024
    return pl.pallas_call(
        _ada_kernel,
        grid=(depth, d3 // bn),
        in_specs=[
            pl.BlockSpec((bsz, d), lambda l, n: (0, 0)),
            pl.BlockSpec((1, d, bn), lambda l, n: (l, 0, n)),
            pl.BlockSpec((1, 1, bn), lambda l, n: (l, 0, n)),
        ],
        out_specs=pl.BlockSpec((1, bsz, bn), lambda l, n: (l, 0, n)),
        out_shape=jax.ShapeDtypeStruct((depth, bsz, d3), F32),
        compiler_params=pltpu.CompilerParams(
            dimension_semantics=("arbitrary", "arbitrary"), vmem_limit_bytes=VMEM_LIMIT),
        name="ada_mod",
    )(c, w_ada, b_ada.reshape(depth, 1, d3))


def _in_kernel(x_ref, g_ref, sc_ref, sh_ref, w_ref, o_ref, h_scr, *, bm, rows, q_tile, q_mult):
    n = pl.program_id(2)

    @pl.when(n == 0)
    def _():
        gm = g_ref[...] * (1.0 + sc_ref[0])
        sh = sh_ref[0]

        def body(r, carry):
            r0 = pl.multiple_of(r * rows, rows)
            xs = x_ref[0, pl.ds(r0, rows), :]
            ms = jnp.mean(xs * xs, axis=-1, keepdims=True)
            y = xs * lax.rsqrt(ms + NORM_EPS)
            h_scr[pl.ds(r0, rows), :] = (y * gm + sh).astype(BF16)
            return carry

        lax.fori_loop(0, bm // rows, body, 0)

    acc = jnp.dot(h_scr[...], w_ref[...], preferred_element_type=F32)
    mult = jnp.where(n == q_tile, jnp.float32(q_mult), jnp.float32(1.0))
    o_ref[0] = (acc * mult).astype(BF16)


def _in_call(x, g_pre, scale, shift, w_in_bf16, *, head_dim):
    bsz, seq, d = x.shape
    in_width = w_in_bf16.shape[1]
    bm, bn = 1024, d // 2
    return pl.pallas_call(
        functools.partial(_in_kernel, bm=bm, rows=16, q_tile=COL_Q,
                          q_mult=(head_dim ** -0.5) * LOG2E),
        grid=(bsz, seq // bm, in_width // bn),
        in_specs=[
            pl.BlockSpec((1, bm, d), lambda b, m, n: (b, m, 0)),
            pl.BlockSpec((1, d), lambda b, m, n: (0, 0)),
            pl.BlockSpec((1, 1, d), lambda b, m, n: (b, 0, 0)),
            pl.BlockSpec((1, 1, d), lambda b, m, n: (b, 0, 0)),
            pl.BlockSpec((d, bn), lambda b, m, n: (0, n)),
        ],
        out_specs=pl.BlockSpec((1, bm, bn), lambda b, m, n: (b, m, n)),
        out_shape=jax.ShapeDtypeStruct((bsz, seq, in_width), BF16),
        scratch_shapes=[pltpu.VMEM((bm, d), BF16)],
        compiler_params=pltpu.CompilerParams(
            dimension_semantics=("arbitrary", "arbitrary", "arbitrary"),
            vmem_limit_bytes=VMEM_LIMIT),
        name="in_proj",
    )(x, g_pre.reshape(1, d), scale, shift, w_in_bf16)


def _attn_kernel(slope_ref, q_ref, k_ref, v_ref, az_ref, lam_ref, g_ref, o_ref,
                 vt_scr, rel_scr, reld_scr, q1_scr, q2_scr, acc1_scr, acc2_scr,
                 *, blk, nkv, half, lam0):
    h = pl.program_id(1)
    i = pl.program_id(2)
    slope = slope_ref[h]

    @pl.when(i == 0)
    def _():
        for j in range(nkv):
            vt_scr[j] = v_ref[0, j * blk:(j + 1) * blk, :].T
        ki = lax.broadcasted_iota(jnp.int32, (blk, blk), 0)
        qi = lax.broadcasted_iota(jnp.int32, (blk, blk), 1)
        rel = (ki - qi).astype(F32) * slope
        rel_scr[...] = rel
        reld_scr[...] = jnp.where(ki <= qi, rel, -jnp.inf)

    qt = q_ref[0].astype(F32).T
    row = lax.broadcasted_iota(jnp.int32, qt.shape, 0)
    q1_scr[...] = jnp.where(row < half, qt, 0.0).astype(BF16)
    q2_scr[...] = jnp.where(row >= half, qt, 0.0).astype(BF16)
    acc1_scr[...] = jnp.zeros_like(acc1_scr)
    acc2_scr[...] = jnp.zeros_like(acc2_scr)

    def one_map(k, vt, bias, off, q_scr, acc_scr, m_old, l_old):
        s = jnp.dot(k, q_scr[...], preferred_element_type=F32) + bias
        m_new = jnp.maximum(m_old, jnp.max(s, axis=0, keepdims=True) + off)
        alpha = jnp.exp2(m_old - m_new)
        p = jnp.exp2(s + (off - m_new))
        l_new = alpha * l_old + jnp.sum(p, axis=0, keepdims=True)
        acc_scr[...] = alpha * acc_scr[...] + jnp.dot(vt, p.astype(BF16),
                                                      preferred_element_type=F32)
        return m_new, l_new

    def tile(j, carry, bias_ref):
        m1, l1, m2, l2 = carry
        k0 = pl.multiple_of(j * blk, blk)
        k = k_ref[0, pl.ds(k0, blk), :]
        vt = vt_scr[j]
        bias = bias_ref[...]
        off = slope * ((j - i) * blk).astype(F32)
        m1, l1 = one_map(k, vt, bias, off, q1_scr, acc1_scr, m1, l1)
        m2, l2 = one_map(k, vt, bias, off, q2_scr, acc2_scr, m2, l2)
        return m1, l1, m2, l2

    neg = jnp.full((1, blk), -jnp.inf, F32)
    zero = jnp.zeros((1, blk), F32)
    carry = lax.fori_loop(0, i, lambda j, c: tile(j, c, rel_scr), (neg, zero, neg, zero))
    m1, l1, m2, l2 = tile(i, carry, reld_scr)

    lq1, lk1, lq2, lk2 = lam_ref[0:1, :], lam_ref[1:2, :], lam_ref[2:3, :], lam_ref[3:4, :]
    lam = (jnp.exp(jnp.sum(lq1 * lk1, axis=-1, keepdims=True))
           - jnp.exp(jnp.sum(lq2 * lk2, axis=-1, keepdims=True)) + lam0)
    ot = acc1_scr[...] * (1.0 / l1) - lam * (acc2_scr[...] * (1.0 / l2))
    ms = jnp.mean(ot * ot, axis=0, keepdims=True)
    ot = ot * lax.rsqrt(ms + NORM_EPS) * g_ref[...] * (1.0 - lam0)
    az = az_ref[0].astype(F32)
    o_ref[0] = (ot.T * _silu(az)).astype(BF16)


def _attn_call(p, lam_vecs, subln_g, slopes, *, layer_idx, d):
    bsz, seq, _ = p.shape
    blk = 512
    nkv = seq // blk
    v_dim = d // (2 * N_HEADS)
    assert v_dim == LANES
    cw = d // 2
    col = lambda c: c * cw // v_dim
    return pl.pallas_call(
        functools.partial(_attn_kernel, blk=blk, nkv=nkv, half=v_dim // 2,
                          lam0=_lambda_init(layer_idx)),
        grid=(bsz, N_HEADS, nkv),
        in_specs=[
            pl.BlockSpec(memory_space=pltpu.SMEM),
            pl.BlockSpec((1, blk, v_dim), lambda b, h, i: (b, i, col(COL_Q) + h)),
            pl.BlockSpec((1, seq, v_dim), lambda b, h, i: (b, 0, col(COL_K) + h)),
            pl.BlockSpec((1, seq, v_dim), lambda b, h, i: (b, 0, col(COL_V) + h)),
            pl.BlockSpec((1, blk, v_dim), lambda b, h, i: (b, i, col(COL_AZ) + h)),
            pl.BlockSpec((4, v_dim // 2), lambda b, h, i: (0, 0)),
            pl.BlockSpec((v_dim, 1), lambda b, h, i: (0, 0)),
        ],
        out_specs=pl.BlockSpec((1, blk, v_dim), lambda b, h, i: (b, i, h)),
        out_shape=jax.ShapeDtypeStruct((bsz, seq, N_HEADS * v_dim), BF16),
        scratch_shapes=[
            pltpu.VMEM((nkv, v_dim, blk), BF16),
            pltpu.VMEM((blk, blk), F32),
            pltpu.VMEM((blk, blk), F32),
            pltpu.VMEM((v_dim, blk), BF16),
            pltpu.VMEM((v_dim, blk), BF16),
            pltpu.VMEM((v_dim, blk), F32),
            pltpu.VMEM((v_dim, blk), F32),
        ],
        compiler_params=pltpu.CompilerParams(
            dimension_semantics=("arbitrary", "arbitrary", "arbitrary"),
            vmem_limit_bytes=VMEM_LIMIT),
        name="diff_attn",
    )(slopes, p, p, p, p, lam_vecs, subln_g.reshape(v_dim, 1))


def _conv_kernel(cv_ref, cg_ref, cvh_ref, cgh_ref, cz_ref, w_ref, b_ref, g_ref, be_ref, o_ref,
                 u_scr, *, bm, halo, rows):
    i = pl.program_id(1)
    uh = cvh_ref[0].astype(F32) * _sigmoid(cgh_ref[0].astype(F32))
    u_scr[0:halo, :] = jnp.where(i > 0, uh, 0.0)
    u_scr[halo:halo + bm, :] = cv_ref[0].astype(F32) * _sigmoid(cg_ref[0].astype(F32))
    first = halo - (CONV_KERNEL - 1)
    for r0 in range(0, bm, rows):
        acc = jnp.zeros((rows, u_scr.shape[1]), F32)
        for t in range(CONV_KERNEL):
            acc = acc + u_scr[r0 + first + t:r0 + first + t + rows, :] * w_ref[t:t + 1, :]
        acc = acc + b_ref[...]
        mu = jnp.mean(acc, axis=-1, keepdims=True)
        cen = acc - mu
        var = jnp.mean(cen * cen, axis=-1, keepdims=True)
        y = cen * lax.rsqrt(var + NORM_EPS) * g_ref[...] + be_ref[...]
        cz = cz_ref[0, r0:r0 + rows, :].astype(F32)
        o_ref[0, r0:r0 + rows, :] = (_silu(y) * _silu(cz)).astype(BF16)


def _conv_call(p, conv_w, conv_b, cn_g, cn_b):
    bsz, seq, _ = p.shape
    cw = conv_w.shape[1]
    bm, halo = 256, 32
    assert halo >= CONV_KERNEL - 1 and bm % halo == 0
    hb = bm // halo
    cur = lambda c: pl.BlockSpec((1, bm, cw), lambda b, i: (b, i, c))
    prev = lambda c: pl.BlockSpec((1, halo, cw), lambda b, i: (b, jnp.maximum(i * hb - 1, 0), c))
    vec = pl.BlockSpec((1, cw), lambda b, i: (0, 0))
    return pl.pallas_call(
        functools.partial(_conv_kernel, bm=bm, halo=halo, rows=32),
        grid=(bsz, seq // bm),
        in_specs=[cur(COL_CV), cur(COL_CG), prev(COL_CV), prev(COL_CG), cur(COL_CZ),
                  pl.BlockSpec((CONV_KERNEL, cw), lambda b, i: (0, 0)), vec, vec, vec],
        out_specs=pl.BlockSpec((1, bm, cw), lambda b, i: (b, i, 0)),
        out_shape=jax.ShapeDtypeStruct((bsz, seq, cw), BF16),
        scratch_shapes=[pltpu.VMEM((halo + bm, cw), F32)],
        compiler_params=pltpu.CompilerParams(
            dimension_semantics=("arbitrary", "arbitrary"), vmem_limit_bytes=VMEM_LIMIT),
        name="conv_branch",
    )(p, p, p, p, p, conv_w, conv_b.reshape(1, cw), cn_g.reshape(1, cw), cn_b.reshape(1, cw))


def _out_kernel(ac_ref, aa_ref, mc0_ref, mc1_ref, ma0_ref, ma1_ref, x_ref, gate_ref,
                wc_ref, wa_ref, wo_ref, g_ref, o_ref):
    yc = jnp.dot(ac_ref[0], wc_ref[...], preferred_element_type=F32)
    ya = jnp.dot(aa_ref[0], wa_ref[...], preferred_element_type=F32)
    mc = jnp.concatenate([mc0_ref[0], mc1_ref[0]], axis=-1).astype(F32)
    ma = jnp.concatenate([ma0_ref[0], ma1_ref[0]], axis=-1).astype(F32)
    merged = _sigmoid(mc) * yc + _sigmoid(ma) * ya
    out = jnp.dot(merged.astype(BF16), wo_ref[...], preferred_element_type=F32)
    ms = jnp.mean(out * out, axis=-1, keepdims=True)
    y = out * lax.rsqrt(ms + NORM_EPS) * g_ref[...]
    o_ref[0] = x_ref[0] + gate_ref[0] * y


def _out_call(p, a_conv, a_attn, x, gate, wc, wa, wo, g_post):
    bsz, seq, d = x.shape
    cw = d // 2
    bm = 256
    resident = lambda shape: pl.BlockSpec(shape, lambda b, i: (0, 0),
                                          pipeline_mode=pl.Buffered(1))
    seg = lambda c: pl.BlockSpec((1, bm, cw), lambda b, i: (b, i, c))
    return pl.pallas_call(
        _out_kernel,
        grid=(bsz, seq // bm),
        in_specs=[
            pl.BlockSpec((1, bm, cw), lambda b, i: (b, i, 0)),
            pl.BlockSpec((1, bm, cw), lambda b, i: (b, i, 0)),
            seg(COL_MC), seg(COL_MC + 1), seg(COL_MA), seg(COL_MA + 1),
            pl.BlockSpec((1, bm, d), lambda b, i: (b, i, 0)),
            pl.BlockSpec((1, 1, d), lambda b, i: (b, 0, 0)),
            resident((cw, d)), resident((cw, d)), resident((d, d)),
            pl.BlockSpec((1, d), lambda b, i: (0, 0)),
        ],
        out_specs=pl.BlockSpec((1, bm, d), lambda b, i: (b, i, 0)),
        out_shape=jax.ShapeDtypeStruct((bsz, seq, d), F32),
        compiler_params=pltpu.CompilerParams(
            dimension_semantics=("arbitrary", "arbitrary"), vmem_limit_bytes=VMEM_LIMIT),
        name="merge_out",
    )(a_conv, a_attn, p, p, p, p, x, gate, wc, wa, wo, g_post.reshape(1, d))


def kernel(x, c, w_ada, b_ada, g_pre, g_post, w_in, conv_w, conv_b, cn_g, cn_b, w_conv_out,
           lam_q1, lam_k1, lam_q2, lam_k2, subln_g, w_attn_out, w_o):
    bsz, seq, d = x.shape
    depth = w_ada.shape[0]
    head_dim = lam_q1.shape[1]
    mod = _ada_call(c, w_ada, b_ada).reshape(depth, bsz, 3, 1, d)
    slopes = jnp.asarray(
        (2.0 ** (-8.0 * np.arange(1, N_HEADS + 1) / N_HEADS)) * LOG2E, dtype=F32)
    for l in range(depth):
        p = _in_call(x, g_pre[l], mod[l, :, 1], mod[l, :, 0], w_in[l].astype(BF16),
                     head_dim=head_dim)
        lam_vecs = jnp.stack([lam_q1[l], lam_k1[l], lam_q2[l], lam_k2[l]])
        a_attn = _attn_call(p, lam_vecs, subln_g[l], slopes, layer_idx=l, d=d)
        a_conv = _conv_call(p, conv_w[l], conv_b[l], cn_g[l], cn_b[l])
        x = _out_call(p, a_conv, a_attn, x, mod[l, :, 2], w_conv_out[l].astype(BF16),
                      w_attn_out[l].astype(BF16), w_o[l].astype(BF16), g_post[l])
    return x
```

```python
import functools
import math

import jax
import jax.numpy as jnp
import numpy as np
from jax import lax
from jax.experimental import pallas as pl
from jax.experimental.pallas import tpu as pltpu

F32 = jnp.float32
BF16 = jnp.bfloat16

NORM_EPS = 1e-6
CONV_KERNEL = 31
N_HEADS = 8
LOG2E = math.log2(math.e)

LANES = 128
V7X_VMEM_BYTES = 64 * 1024 * 1024
VMEM_LIMIT = V7X_VMEM_BYTES - 8 * 1024 * 1024

COL_CV, COL_CG, COL_CZ, COL_Q, COL_K, COL_V, COL_AZ, COL_MC, COL_MA = 0, 1, 2, 3, 4, 5, 6, 7, 9


def _lambda_init(layer_idx):
    return 0.8 - 0.6 * math.exp(-0.3 * layer_idx)


def _sigmoid(v):
    return 1.0 / (1.0 + jnp.exp(-v))


def _silu(v):
    return v * _sigmoid(v)


def _ada_kernel(c_ref, w_ref, b_ref, o_ref):
    c = c_ref[...]
    ca = _silu(c).astype(BF16)
    o_ref[0] = jnp.dot(ca, w_ref[0].astype(BF16), preferred_element_type=F32) + b_ref[0]


def _ada_call(c, w_ada, b_ada):
    depth, d, d3 = w_ada.shape
    bsz = c.shape[0]
    bn = 1024
    return pl.pallas_call(
        _ada_kernel,
        grid=(depth, d3 // bn),
        in_specs=[
            pl.BlockSpec((bsz, d), lambda l, n: (0, 0)),
            pl.BlockSpec((1, d, bn), lambda l, n: (l, 0, n)),
            pl.BlockSpec((1, 1, bn), lambda l, n: (l, 0, n)),
        ],
        out_specs=pl.BlockSpec((1, bsz, bn), lambda l, n: (l, 0, n)),
        out_shape=jax.ShapeDtypeStruct((depth, bsz, d3), F32),
        compiler_params=pltpu.CompilerParams(
            dimension_semantics=("arbitrary", "arbitrary"), vmem_limit_bytes=VMEM_LIMIT),
        name="ada_mod",
    )(c, w_ada, b_ada.reshape(depth, 1, d3))


def _in_kernel(x_ref, g_ref, sc_ref, sh_ref, w_ref, o_ref, h_scr, *, bm, rows, q_tile, q_mult):
    n = pl.program_id(2)

    @pl.when(n == 0)
    def _():
        gm = g_ref[...] * (1.0 + sc_ref[0])
        sh = sh_ref[0]

        def body(r, carry):
            r0 = pl.multiple_of(r * rows, rows)
            xs = x_ref[0, pl.ds(r0, rows), :]
            ms = jnp.mean(xs * xs, axis=-1, keepdims=True)
            y = xs * lax.rsqrt(ms + NORM_EPS)
            h_scr[pl.ds(r0, rows), :] = (y * gm + sh).astype(BF16)
            return carry

        lax.fori_loop(0, bm // rows, body, 0)

    acc = jnp.dot(h_scr[...], w_ref[...], preferred_element_type=F32)
    mult = jnp.where(n == q_tile, jnp.float32(q_mult), jnp.float32(1.0))
    o_ref[0] = (acc * mult).astype(BF16)


def _in_call(x, g_pre, scale, shift, w_in_bf16, *, head_dim):
    bsz, seq, d = x.shape
    in_width = w_in_bf16.shape[1]
    bm, bn = 1024, d // 2
    return pl.pallas_call(
        functools.partial(_in_kernel, bm=bm, rows=16, q_tile=COL_Q,
                          q_mult=(head_dim ** -0.5) * LOG2E),
        grid=(bsz, seq // bm, in_width // bn),
        in_specs=[
            pl.BlockSpec((1, bm, d), lambda b, m, n: (b, m, 0)),
            pl.BlockSpec((1, d), lambda b, m, n: (0, 0)),
            pl.BlockSpec((1, 1, d), lambda b, m, n: (b, 0, 0)),
            pl.BlockSpec((1, 1, d), lambda b, m, n: (b, 0, 0)),
            pl.BlockSpec((d, bn), lambda b, m, n: (0, n)),
        ],
        out_specs=pl.BlockSpec((1, bm, bn), lambda b, m, n: (b, m, n)),
        out_shape=jax.ShapeDtypeStruct((bsz, seq, in_width), BF16),
        scratch_shapes=[pltpu.VMEM((bm, d), BF16)],
        compiler_params=pltpu.CompilerParams(
            dimension_semantics=("arbitrary", "arbitrary", "arbitrary"),
            vmem_limit_bytes=VMEM_LIMIT),
        name="in_proj",
    )(x, g_pre.reshape(1, d), scale, shift, w_in_bf16)


ONES_ROWS = 16
N_POS = 9


def _split3_bf16(v):
    p1 = v.astype(BF16).astype(F32)
    p2 = (v - p1).astype(BF16).astype(F32)
    p3 = (v - p1 - p2).astype(BF16).astype(F32)
    return p1, p2, p3


def _attn_kernel(slope_ref, q_ref, k_ref, v_ref, az_ref, lam_ref, g_ref, o_ref,
                 vt_scr, pos_scr, q1_scr, q2_scr, acc1_scr, acc2_scr,
                 sa1_scr, sa2_scr, sb1_scr, sb2_scr, *, blk, nkv, half, lam0):
    h = pl.program_id(1)
    i = pl.program_id(2)
    slope = slope_ref[h]
    vd = 2 * half

    @pl.when(i == 0)
    def _():
        for j in range(nkv):
            vt_scr[j, 0:vd, :] = v_ref[0, j * blk:(j + 1) * blk, :].T
            vt_scr[j, vd:, :] = jnp.ones((ONES_ROWS, blk), BF16)
        ki = lax.broadcasted_iota(jnp.int32, (blk, vd), 0)
        col = lax.broadcasted_iota(jnp.int32, (blk, vd), 1)
        lo = ki & 1
        feat = jnp.where(col < 3, ki - lo, jnp.where(col < 6, lo, jnp.where(col < N_POS, 1, 0)))
        pos_scr[...] = feat.astype(F32).astype(BF16)
        cvec = jnp.full((1, blk), slope, F32)
        c1, c2, c3 = _split3_bf16(cvec)
        qi = lax.broadcasted_iota(jnp.int32, (1, blk), 1).astype(F32)
        r1, r2, r3 = _split3_bf16(-(cvec * qi))
        row = lax.broadcasted_iota(jnp.int32, (vd, blk), 0)
        rows = jnp.zeros((vd, blk), F32)
        for r, val in enumerate((c1, c2, c3, c1, c2, c3, r1, r2, r3)):
            rows = jnp.where(row == r, val, rows)
        q1_scr[vd:, :] = rows.astype(BF16)
        q2_scr[vd:, :] = rows.astype(BF16)

    qt = q_ref[0].astype(F32).T
    row = lax.broadcasted_iota(jnp.int32, qt.shape, 0)
    q1_scr[0:vd, :] = jnp.where(row < half, qt, 0.0).astype(BF16)
    q2_scr[0:vd, :] = jnp.where(row >= half, qt, 0.0).astype(BF16)
    acc1_scr[...] = jnp.zeros_like(acc1_scr)
    acc2_scr[...] = jnp.zeros_like(acc2_scr)

    def scores(j, q_scr, s_ref):
        k = k_ref[0, pl.ds(pl.multiple_of(j * blk, blk), blk), :]
        ka = jnp.concatenate([k, pos_scr[...]], axis=1)
        s_ref[...] = jnp.dot(ka, q_scr[...], preferred_element_type=F32)

    def one_map(s, vt, off, acc_scr, m_old):
        m_new = jnp.maximum(m_old, jnp.max(s, axis=0, keepdims=True) + off)
        alpha = jnp.exp2(m_old - m_new)
        p = jnp.exp2(s + (off - m_new)).astype(BF16)
        acc_scr[...] = alpha * acc_scr[...] + jnp.dot(vt, p, preferred_element_type=F32)
        return m_new

    def step(j, cur, nxt, carry):
        m1, m2 = carry
        vt = vt_scr[j]
        off = slope * ((j - i) * blk).astype(F32)
        scores(j + 1, q1_scr, nxt[0])
        m1 = one_map(cur[0][...], vt, off, acc1_scr, m1)
        scores(j + 1, q2_scr, nxt[1])
        m2 = one_map(cur[1][...], vt, off, acc2_scr, m2)
        return m1, m2

    buf_a, buf_b = (sa1_scr, sa2_scr), (sb1_scr, sb2_scr)
    odd = (i % 2) == 1
    first = jnp.where(odd, 1, 0)
    neg = jnp.full((1, blk), -jnp.inf, F32)
    scores(0, q1_scr, sa1_scr)
    scores(0, q2_scr, sa2_scr)

    def peel(carry):
        m1, m2 = carry
        vt = vt_scr[0]
        off = slope * (-i * blk).astype(F32)
        m1 = one_map(sa1_scr[...], vt, off, acc1_scr, m1)
        m2 = one_map(sa2_scr[...], vt, off, acc2_scr, m2)
        scores(1, q1_scr, sa1_scr)
        scores(1, q2_scr, sa2_scr)
        return m1, m2

    carry = lax.cond(odd, peel, lambda c: c, (neg, neg))

    def pair(t, carry):
        a = first + 2 * t
        carry = step(a, buf_a, buf_b, carry)
        return step(a + 1, buf_b, buf_a, carry)

    m1, m2 = lax.fori_loop(0, (i - first) // 2, pair, carry)

    causal = (lax.broadcasted_iota(jnp.int32, (blk, blk), 0)
              <= lax.broadcasted_iota(jnp.int32, (blk, blk), 1))
    vt = vt_scr[i]
    one_map(jnp.where(causal, sa1_scr[...], -jnp.inf), vt, 0.0, acc1_scr, m1)
    one_map(jnp.where(causal, sa2_scr[...], -jnp.inf), vt, 0.0, acc2_scr, m2)

    lq1, lk1, lq2, lk2 = lam_ref[0:1, :], lam_ref[1:2, :], lam_ref[2:3, :], lam_ref[3:4, :]
    lam = (jnp.exp(jnp.sum(lq1 * lk1, axis=-1, keepdims=True))
           - jnp.exp(jnp.sum(lq2 * lk2, axis=-1, keepdims=True)) + lam0)
    inv1 = 1.0 / acc1_scr[vd:vd + 1, :]
    inv2 = 1.0 / acc2_scr[vd:vd + 1, :]
    ot = acc1_scr[0:vd, :] * inv1 - lam * (acc2_scr[0:vd, :] * inv2)
    ms = jnp.mean(ot * ot, axis=0, keepdims=True)
    ot = ot * lax.rsqrt(ms + NORM_EPS) * g_ref[...] * (1.0 - lam0)
    az = az_ref[0].astype(F32)
    o_ref[0] = (ot.T * _silu(az)).astype(BF16)


def _attn_call(p, lam_vecs, subln_g, slopes, *, layer_idx, d):
    bsz, seq, _ = p.shape
    blk = 512
    nkv = seq // blk
    v_dim = d // (2 * N_HEADS)
    assert v_dim == LANES
    cw = d // 2
    col = lambda c: c * cw // v_dim
    return pl.pallas_call(
        functools.partial(_attn_kernel, blk=blk, nkv=nkv, half=v_dim // 2,
                          lam0=_lambda_init(layer_idx)),
        grid=(bsz, N_HEADS, nkv),
        in_specs=[
            pl.BlockSpec(memory_space=pltpu.SMEM),
            pl.BlockSpec((1, blk, v_dim), lambda b, h, i: (b, i, col(COL_Q) + h)),
            pl.BlockSpec((1, seq, v_dim), lambda b, h, i: (b, 0, col(COL_K) + h)),
            pl.BlockSpec((1, seq, v_dim), lambda b, h, i: (b, 0, col(COL_V) + h)),
            pl.BlockSpec((1, blk, v_dim), lambda b, h, i: (b, i, col(COL_AZ) + h)),
            pl.BlockSpec((4, v_dim // 2), lambda b, h, i: (0, 0)),
            pl.BlockSpec((v_dim, 1), lambda b, h, i: (0, 0)),
        ],
        out_specs=pl.BlockSpec((1, blk, v_dim), lambda b, h, i: (b, i, h)),
        out_shape=jax.ShapeDtypeStruct((bsz, seq, N_HEADS * v_dim), BF16),
        scratch_shapes=[
            pltpu.VMEM((nkv, v_dim + ONES_ROWS, blk), BF16),
            pltpu.VMEM((blk, v_dim), BF16),
            pltpu.VMEM((2 * v_dim, blk), BF16),
            pltpu.VMEM((2 * v_dim, blk), BF16),
            pltpu.VMEM((v_dim + ONES_ROWS, blk), F32),
            pltpu.VMEM((v_dim + ONES_ROWS, blk), F32),
            pltpu.VMEM((blk, blk), F32), pltpu.VMEM((blk, blk), F32),
            pltpu.VMEM((blk, blk), F32), pltpu.VMEM((blk, blk), F32),
        ],
        compiler_params=pltpu.CompilerParams(
            dimension_semantics=("arbitrary", "arbitrary", "arbitrary"),
            vmem_limit_bytes=VMEM_LIMIT),
        name="diff_attn",
    )(slopes, p, p, p, p, lam_vecs, subln_g.reshape(v_dim, 1))


def _conv_kernel(cv_ref, cg_ref, cvh_ref, cgh_ref, cz_ref, w_ref, b_ref, g_ref, be_ref, o_ref,
                 u_scr, *, bm, halo, rows):
    i = pl.program_id(1)
    uh = cvh_ref[0].astype(F32) * _sigmoid(cgh_ref[0].astype(F32))
    u_scr[0:halo, :] = jnp.where(i > 0, uh, 0.0)
    u_scr[halo:halo + bm, :] = cv_ref[0].astype(F32) * _sigmoid(cg_ref[0].astype(F32))
    first = halo - (CONV_KERNEL - 1)
    for r0 in range(0, bm, rows):
        acc = jnp.zeros((rows, u_scr.shape[1]), F32)
        for t in range(CONV_KERNEL):
            acc = acc + u_scr[r0 + first + t:r0 + first + t + rows, :] * w_ref[t:t + 1, :]
        acc = acc + b_ref[...]
        mu = jnp.mean(acc, axis=-1, keepdims=True)
        cen = acc - mu
        var = jnp.mean(cen * cen, axis=-1, keepdims=True)
        y = cen * lax.rsqrt(var + NORM_EPS) * g_ref[...] + be_ref[...]
        cz = cz_ref[0, r0:r0 + rows, :].astype(F32)
        o_ref[0, r0:r0 + rows, :] = (_silu(y) * _silu(cz)).astype(BF16)


def _conv_call(p, conv_w, conv_b, cn_g, cn_b):
    bsz, seq, _ = p.shape
    cw = conv_w.shape[1]
    bm, halo = 256, 32
    assert halo >= CONV_KERNEL - 1 and bm % halo == 0
    hb = bm // halo
    cur = lambda c: pl.BlockSpec((1, bm, cw), lambda b, i: (b, i, c))
    prev = lambda c: pl.BlockSpec((1, halo, cw), lambda b, i: (b, jnp.maximum(i * hb - 1, 0), c))
    vec = pl.BlockSpec((1, cw), lambda b, i: (0, 0))
    return pl.pallas_call(
        functools.partial(_conv_kernel, bm=bm, halo=halo, rows=32),
        grid=(bsz, seq // bm),
        in_specs=[cur(COL_CV), cur(COL_CG), prev(COL_CV), prev(COL_CG), cur(COL_CZ),
                  pl.BlockSpec((CONV_KERNEL, cw), lambda b, i: (0, 0)), vec, vec, vec],
        out_specs=pl.BlockSpec((1, bm, cw), lambda b, i: (b, i, 0)),
        out_shape=jax.ShapeDtypeStruct((bsz, seq, cw), BF16),
        scratch_shapes=[pltpu.VMEM((halo + bm, cw), F32)],
        compiler_params=pltpu.CompilerParams(
            dimension_semantics=("arbitrary", "arbitrary"), vmem_limit_bytes=VMEM_LIMIT),
        name="conv_branch",
    )(p, p, p, p, p, conv_w, conv_b.reshape(1, cw), cn_g.reshape(1, cw), cn_b.reshape(1, cw))


def _out_kernel(ac_ref, aa_ref, mc0_ref, mc1_ref, ma0_ref, ma1_ref, x_ref, gate_ref,
                wc_ref, wa_ref, wo_ref, g_ref, o_ref):
    yc = jnp.dot(ac_ref[0], wc_ref[...], preferred_element_type=F32)
    ya = jnp.dot(aa_ref[0], wa_ref[...], preferred_element_type=F32)
    mc = jnp.concatenate([mc0_ref[0], mc1_ref[0]], axis=-1).astype(F32)
    ma = jnp.concatenate([ma0_ref[0], ma1_ref[0]], axis=-1).astype(F32)
    merged = _sigmoid(mc) * yc + _sigmoid(ma) * ya
    out = jnp.dot(merged.astype(BF16), wo_ref[...], preferred_element_type=F32)
    ms = jnp.mean(out * out, axis=-1, keepdims=True)
    y = out * lax.rsqrt(ms + NORM_EPS) * g_ref[...]
    o_ref[0] = x_ref[0] + gate_ref[0] * y


def _out_call(p, a_conv, a_attn, x, gate, wc, wa, wo, g_post):
    bsz, seq, d = x.shape
    cw = d // 2
    bm = 256
    resident = lambda shape: pl.BlockSpec(shape, lambda b, i: (0, 0),
                                          pipeline_mode=pl.Buffered(1))
    seg = lambda c: pl.BlockSpec((1, bm, cw), lambda b, i: (b, i, c))
    return pl.pallas_call(
        _out_kernel,
        grid=(bsz, seq // bm),
        in_specs=[
            pl.BlockSpec((1, bm, cw), lambda b, i: (b, i, 0)),
            pl.BlockSpec((1, bm, cw), lambda b, i: (b, i, 0)),
            seg(COL_MC), seg(COL_MC + 1), seg(COL_MA), seg(COL_MA + 1),
            pl.BlockSpec((1, bm, d), lambda b, i: (b, i, 0)),
            pl.BlockSpec((1, 1, d), lambda b, i: (b, 0, 0)),
            resident((cw, d)), resident((cw, d)), resident((d, d)),
            pl.BlockSpec((1, d), lambda b, i: (0, 0)),
        ],
        out_specs=pl.BlockSpec((1, bm, d), lambda b, i: (b, i, 0)),
        out_shape=jax.ShapeDtypeStruct((bsz, seq, d), F32),
        compiler_params=pltpu.CompilerParams(
            dimension_semantics=("arbitrary", "arbitrary"), vmem_limit_bytes=VMEM_LIMIT),
        name="merge_out",
    )(a_conv, a_attn, p, p, p, p, x, gate, wc, wa, wo, g_post.reshape(1, d))


def kernel(x, c, w_ada, b_ada, g_pre, g_post, w_in, conv_w, conv_b, cn_g, cn_b, w_conv_out,
           lam_q1, lam_k1, lam_q2, lam_k2, subln_g, w_attn_out, w_o):
    bsz, seq, d = x.shape
    depth = w_ada.shape[0]
    head_dim = lam_q1.shape[1]
    mod = _ada_call(c, w_ada, b_ada).reshape(depth, bsz, 3, 1, d)
    slopes = jnp.asarray(
        (2.0 ** (-8.0 * np.arange(1, N_HEADS + 1) / N_HEADS)) * LOG2E, dtype=F32)
    for l in range(depth):
        p = _in_call(x, g_pre[l], mod[l, :, 1], mod[l, :, 0], w_in[l].astype(BF16),
                     head_dim=head_dim)
        lam_vecs = jnp.stack([lam_q1[l], lam_k1[l], lam_q2[l], lam_k2[l]])
        a_attn = _attn_call(p, lam_vecs, subln_g[l], slopes, layer_idx=l, d=d)
        a_conv = _conv_call(p, conv_w[l], conv_b[l], cn_g[l], cn_b[l])
        x = _out_call(p, a_conv, a_attn, x, mod[l, :, 2], w_conv_out[l].astype(BF16),
                      w_attn_out[l].astype(BF16), w_o[l].astype(BF16), g_post[l])
    return x
```

```python
import functools
import math

import jax
import jax.numpy as jnp
import numpy as np
from jax import lax
from jax.experimental import pallas as pl
from jax.experimental.pallas import tpu as pltpu

F32 = jnp.float32
BF16 = jnp.bfloat16

NORM_EPS = 1e-6
CONV_KERNEL = 31
N_HEADS = 8
LOG2E = math.log2(math.e)

LANES = 128
SUBLANES = 8
V7X_VMEM_BYTES = 64 * 1024 * 1024
VMEM_LIMIT = V7X_VMEM_BYTES - 8 * 1024 * 1024

COL_CV, COL_CG, COL_CZ, COL_Q, COL_K, COL_V, COL_AZ, COL_MC, COL_MA = 0, 1, 2, 3, 4, 5, 6, 7, 9


def _lambda_init(layer_idx):
    return 0.8 - 0.6 * math.exp(-0.3 * layer_idx)


def _sigmoid(v):
    return 1.0 / (1.0 + jnp.exp(-v))


def _silu(v):
    return v * _sigmoid(v)


def _ada_kernel(c_ref, w_ref, b_ref, o_ref):
    c = c_ref[...]
    ca = _silu(c).astype(BF16)
    o_ref[0] = jnp.dot(ca, w_ref[0].astype(BF16), preferred_element_type=F32) + b_ref[0]


def _ada_call(c, w_ada, b_ada):
    depth, d, d3 = w_ada.shape
    bsz = c.shape[0]
    bn = 1024
    return pl.pallas_call(
        _ada_kernel,
        grid=(depth, d3 // bn),
        in_specs=[
            pl.BlockSpec((bsz, d), lambda l, n: (0, 0)),
            pl.BlockSpec((1, d, bn), lambda l, n: (l, 0, n)),
            pl.BlockSpec((1, 1, bn), lambda l, n: (l, 0, n)),
        ],
        out_specs=pl.BlockSpec((1, bsz, bn), lambda l, n: (l, 0, n)),
        out_shape=jax.ShapeDtypeStruct((depth, bsz, d3), F32),
        compiler_params=pltpu.CompilerParams(
            dimension_semantics=("arbitrary", "arbitrary"), vmem_limit_bytes=VMEM_LIMIT),
        name="ada_mod",
    )(c, w_ada, b_ada.reshape(depth, 1, d3))


def _in_kernel(x_ref, g_ref, sc_ref, sh_ref, w_ref, o_ref, h_scr, *, bm, rows, q_tile, q_mult):
    n = pl.program_id(2)

    @pl.when(n == 0)
    def _():
        gm = g_ref[...] * (1.0 + sc_ref[0])
        sh = sh_ref[0]

        def body(r, carry):
            r0 = pl.multiple_of(r * rows, rows)
            xs = x_ref[0, pl.ds(r0, rows), :]
            ms = jnp.mean(xs * xs, axis=-1, keepdims=True)
            y = xs * lax.rsqrt(ms + NORM_EPS)
            h_scr[pl.ds(r0, rows), :] = (y * gm + sh).astype(BF16)
            return carry

        lax.fori_loop(0, bm // rows, body, 0)

    acc = jnp.dot(h_scr[...], w_ref[...], preferred_element_type=F32)
    mult = jnp.where(n == q_tile, jnp.float32(q_mult), jnp.float32(1.0))
    o_ref[0] = (acc * mult).astype(BF16)


def _in_call(x, g_pre, scale, shift, w_in_bf16, *, head_dim):
    bsz, seq, d = x.shape
    in_width = w_in_bf16.shape[1]
    bm, bn = 1024, d // 2
    return pl.pallas_call(
        functools.partial(_in_kernel, bm=bm, rows=16, q_tile=COL_Q,
                          q_mult=(head_dim ** -0.5) * LOG2E),
        grid=(bsz, seq // bm, in_width // bn),
        in_specs=[
            pl.BlockSpec((1, bm, d), lambda b, m, n: (b, m, 0)),
            pl.BlockSpec((1, d), lambda b, m, n: (0, 0)),
            pl.BlockSpec((1, 1, d), lambda b, m, n: (b, 0, 0)),
            pl.BlockSpec((1, 1, d), lambda b, m, n: (b, 0, 0)),
            pl.BlockSpec((d, bn), lambda b, m, n: (0, n)),
        ],
        out_specs=pl.BlockSpec((1, bm, bn), lambda b, m, n: (b, m, n)),
        out_shape=jax.ShapeDtypeStruct((bsz, seq, in_width), BF16),
        scratch_shapes=[pltpu.VMEM((bm, d), BF16)],
        compiler_params=pltpu.CompilerParams(
            dimension_semantics=("arbitrary", "arbitrary", "arbitrary"),
            vmem_limit_bytes=VMEM_LIMIT),
        name="in_proj",
    )(x, g_pre.reshape(1, d), scale, shift, w_in_bf16)


ONES_ROWS = 16
N_POS = 9


def _split3_bf16(v):
    p1 = v.astype(BF16).astype(F32)
    p2 = (v - p1).astype(BF16).astype(F32)
    p3 = (v - p1 - p2).astype(BF16).astype(F32)
    return p1, p2, p3


def _attn_kernel(slope_ref, q_ref, k_ref, v_ref, az_ref, lam_ref, g_ref, o_ref,
                 vt_scr, pos_scr, q1_scr, q2_scr, acc1_scr, acc2_scr,
                 sa1_scr, sa2_scr, sb1_scr, sb2_scr, p1_scr, p2_scr, *, blk, nkv, half, lam0):
    h = pl.program_id(1)
    i = pl.program_id(2)
    slope = slope_ref[h]
    vd = 2 * half

    @pl.when(i == 0)
    def _():
        for j in range(nkv):
            vt_scr[j, 0:vd, :] = v_ref[0, j * blk:(j + 1) * blk, :].T
            vt_scr[j, vd:, :] = jnp.ones((ONES_ROWS, blk), BF16)
        ki = lax.broadcasted_iota(jnp.int32, (blk, vd), 0)
        col = lax.broadcasted_iota(jnp.int32, (blk, vd), 1)
        lo = ki & 1
        feat = jnp.where(col < 3, ki - lo, jnp.where(col < 6, lo, jnp.where(col < N_POS, 1, 0)))
        pos_scr[...] = feat.astype(F32).astype(BF16)
        cvec = jnp.full((1, blk), slope, F32)
        c1, c2, c3 = _split3_bf16(cvec)
        qi = lax.broadcasted_iota(jnp.int32, (1, blk), 1).astype(F32)
        r1, r2, r3 = _split3_bf16(-(cvec * qi))
        row = lax.broadcasted_iota(jnp.int32, (vd, blk), 0)
        rows = jnp.zeros((vd, blk), F32)
        for r, val in enumerate((c1, c2, c3, c1, c2, c3, r1, r2, r3)):
            rows = jnp.where(row == r, val, rows)
        q1_scr[vd:, :] = rows.astype(BF16)
        q2_scr[vd:, :] = rows.astype(BF16)

    qt = q_ref[0].astype(F32).T
    row = lax.broadcasted_iota(jnp.int32, qt.shape, 0)
    q1_scr[0:vd, :] = jnp.where(row < half, qt, 0.0).astype(BF16)
    q2_scr[0:vd, :] = jnp.where(row >= half, qt, 0.0).astype(BF16)
    acc1_scr[...] = jnp.zeros_like(acc1_scr)
    acc2_scr[...] = jnp.zeros_like(acc2_scr)
    p1_scr[...] = jnp.zeros_like(p1_scr)
    p2_scr[...] = jnp.zeros_like(p2_scr)

    def scores(j, q_scr, s_ref):
        k = k_ref[0, pl.ds(pl.multiple_of(j * blk, blk), blk), :]
        ka = jnp.concatenate([k, pos_scr[...]], axis=1)
        s_ref[...] = jnp.dot(ka, q_scr[...], preferred_element_type=F32)

    def softmax(s, off, m_old):
        m_new = jnp.maximum(m_old, jnp.max(s, axis=0, keepdims=True) + off)
        alpha = jnp.exp2(m_old - m_new)
        p = jnp.exp2(s + (off - m_new)).astype(BF16)
        return m_new, alpha, p

    def one_map(s, vt, off, acc_scr, m_old):
        m_new, alpha, p = softmax(s, off, m_old)
        acc_scr[...] = alpha * acc_scr[...] + jnp.dot(vt, p, preferred_element_type=F32)
        return m_new

    def flush(j):
        vt = vt_scr[jnp.maximum(j, 0)]
        acc1_scr[...] += jnp.dot(vt, p1_scr[...], preferred_element_type=F32)
        acc2_scr[...] += jnp.dot(vt, p2_scr[...], preferred_element_type=F32)

    def step(j, cur, nxt, carry):
        m1, m2 = carry
        off = slope * ((j - i) * blk).astype(F32)
        flush(j - 1)
        scores(j + 1, q1_scr, nxt[0])
        scores(j + 1, q2_scr, nxt[1])
        m1, alpha1, p1 = softmax(cur[0][...], off, m1)
        acc1_scr[...] = alpha1 * acc1_scr[...]
        p1_scr[...] = p1
        m2, alpha2, p2 = softmax(cur[1][...], off, m2)
        acc2_scr[...] = alpha2 * acc2_scr[...]
        p2_scr[...] = p2
        return m1, m2

    buf_a, buf_b = (sa1_scr, sa2_scr), (sb1_scr, sb2_scr)
    odd = (i % 2) == 1
    first = jnp.where(odd, 1, 0)
    neg = jnp.full((1, blk), -jnp.inf, F32)
    scores(0, q1_scr, sa1_scr)
    scores(0, q2_scr, sa2_scr)

    def peel(carry):
        m1, m2 = carry
        vt = vt_scr[0]
        off = slope * (-i * blk).astype(F32)
        m1 = one_map(sa1_scr[...], vt, off, acc1_scr, m1)
        m2 = one_map(sa2_scr[...], vt, off, acc2_scr, m2)
        scores(1, q1_scr, sa1_scr)
        scores(1, q2_scr, sa2_scr)
        return m1, m2

    carry = lax.cond(odd, peel, lambda c: c, (neg, neg))

    def pair(t, carry):
        a = first + 2 * t
        carry = step(a, buf_a, buf_b, carry)
        return step(a + 1, buf_b, buf_a, carry)

    m1, m2 = lax.fori_loop(0, (i - first) // 2, pair, carry)

    causal = (lax.broadcasted_iota(jnp.int32, (blk, blk), 0)
              <= lax.broadcasted_iota(jnp.int32, (blk, blk), 1))
    vt = vt_scr[i]
    flush(i - 1)
    one_map(jnp.where(causal, sa1_scr[...], -jnp.inf), vt, 0.0, acc1_scr, m1)
    one_map(jnp.where(causal, sa2_scr[...], -jnp.inf), vt, 0.0, acc2_scr, m2)

    lq1, lk1, lq2, lk2 = lam_ref[0:1, :], lam_ref[1:2, :], lam_ref[2:3, :], lam_ref[3:4, :]
    lam = (jnp.exp(jnp.sum(lq1 * lk1, axis=-1, keepdims=True))
           - jnp.exp(jnp.sum(lq2 * lk2, axis=-1, keepdims=True)) + lam0)
    inv1 = 1.0 / acc1_scr[vd:vd + 1, :]
    inv2 = 1.0 / acc2_scr[vd:vd + 1, :]
    ot = acc1_scr[0:vd, :] * inv1 - lam * (acc2_scr[0:vd, :] * inv2)
    ms = jnp.mean(ot * ot, axis=0, keepdims=True)
    ot = ot * lax.rsqrt(ms + NORM_EPS) * g_ref[...] * (1.0 - lam0)
    az = az_ref[0].astype(F32)
    o_ref[0] = (ot.T * _silu(az)).astype(BF16)


def _attn_call(p, lam_vecs, subln_g, slopes, *, layer_idx, d):
    bsz, seq, _ = p.shape
    blk = 512
    nkv = seq // blk
    v_dim = d // (2 * N_HEADS)
    assert v_dim == LANES
    cw = d // 2
    col = lambda c: c * cw // v_dim
    return pl.pallas_call(
        functools.partial(_attn_kernel, blk=blk, nkv=nkv, half=v_dim // 2,
                          lam0=_lambda_init(layer_idx)),
        grid=(bsz, N_HEADS, nkv),
        in_specs=[
            pl.BlockSpec(memory_space=pltpu.SMEM),
            pl.BlockSpec((1, blk, v_dim), lambda b, h, i: (b, i, col(COL_Q) + h)),
            pl.BlockSpec((1, seq, v_dim), lambda b, h, i: (b, 0, col(COL_K) + h)),
            pl.BlockSpec((1, seq, v_dim), lambda b, h, i: (b, 0, col(COL_V) + h)),
            pl.BlockSpec((1, blk, v_dim), lambda b, h, i: (b, i, col(COL_AZ) + h)),
            pl.BlockSpec((4, v_dim // 2), lambda b, h, i: (0, 0)),
            pl.BlockSpec((v_dim, 1), lambda b, h, i: (0, 0)),
        ],
        out_specs=pl.BlockSpec((1, blk, v_dim), lambda b, h, i: (b, i, h)),
        out_shape=jax.ShapeDtypeStruct((bsz, seq, N_HEADS * v_dim), BF16),
        scratch_shapes=[
            pltpu.VMEM((nkv, v_dim + ONES_ROWS, blk), BF16),
            pltpu.VMEM((blk, v_dim), BF16),
            pltpu.VMEM((2 * v_dim, blk), BF16),
            pltpu.VMEM((2 * v_dim, blk), BF16),
            pltpu.VMEM((v_dim + ONES_ROWS, blk), F32),
            pltpu.VMEM((v_dim + ONES_ROWS, blk), F32),
            pltpu.VMEM((blk, blk), F32), pltpu.VMEM((blk, blk), F32),
            pltpu.VMEM((blk, blk), F32), pltpu.VMEM((blk, blk), F32),
            pltpu.VMEM((blk, blk), BF16), pltpu.VMEM((blk, blk), BF16),
        ],
        compiler_params=pltpu.CompilerParams(
            dimension_semantics=("arbitrary", "arbitrary", "arbitrary"),
            vmem_limit_bytes=VMEM_LIMIT),
        name="diff_attn",
    )(slopes, p, p, p, p, lam_vecs, subln_g.reshape(v_dim, 1))


def _conv_kernel(cv_ref, cg_ref, cvh_ref, cgh_ref, cz_ref, w_ref, b_ref, g_ref, be_ref, o_ref,
                 u_scr, sh_scr, *, bm, halo, rows):
    i = pl.program_id(1)
    uh = cvh_ref[0].astype(F32) * _sigmoid(cgh_ref[0].astype(F32))
    u_scr[0:halo, :] = jnp.where(i > 0, uh, 0.0)
    u_scr[halo:halo + bm, :] = cv_ref[0].astype(F32) * _sigmoid(cg_ref[0].astype(F32))
    n_sh = sh_scr.shape[1]
    u_all = u_scr[...]
    for b in range(1, SUBLANES):
        sh_scr[b - 1] = pltpu.roll(u_all, halo + bm - b, 0)[0:n_sh, :]
    first = halo - (CONV_KERNEL - 1)
    cw = u_scr.shape[1]
    for r0 in range(0, bm, rows):
        acc = jnp.zeros((rows // SUBLANES, SUBLANES, cw), F32)
        for t in range(CONV_KERNEL):
            b = (first + t) % SUBLANES
            a0 = r0 + first + t - b
            tap = u_scr[a0:a0 + rows, :] if b == 0 else sh_scr[b - 1, a0:a0 + rows, :]
            acc = acc + tap.reshape(acc.shape) * w_ref[t]
        acc = acc.reshape(rows, cw) + b_ref[...]
        mu = jnp.mean(acc, axis=-1, keepdims=True)
        cen = acc - mu
        var = jnp.mean(cen * cen, axis=-1, keepdims=True)
        y = cen * lax.rsqrt(var + NORM_EPS) * g_ref[...] + be_ref[...]
        cz = cz_ref[0, r0:r0 + rows, :].astype(F32)
        o_ref[0, r0:r0 + rows, :] = (_silu(y) * _silu(cz)).astype(BF16)


def _conv_call(p, conv_w, conv_b, cn_g, cn_b):
    bsz, seq, _ = p.shape
    cw = conv_w.shape[1]
    bm, halo = 256, 32
    assert halo >= CONV_KERNEL - 1 and bm % halo == 0
    hb = bm // halo
    cur = lambda c: pl.BlockSpec((1, bm, cw), lambda b, i: (b, i, c))
    prev = lambda c: pl.BlockSpec((1, halo, cw), lambda b, i: (b, jnp.maximum(i * hb - 1, 0), c))
    vec = pl.BlockSpec((1, cw), lambda b, i: (0, 0))
    return pl.pallas_call(
        functools.partial(_conv_kernel, bm=bm, halo=halo, rows=32),
        grid=(bsz, seq // bm),
        in_specs=[cur(COL_CV), cur(COL_CG), prev(COL_CV), prev(COL_CG), cur(COL_CZ),
                  pl.BlockSpec((CONV_KERNEL, SUBLANES, cw), lambda b, i: (0, 0, 0)), vec, vec, vec],
        out_specs=pl.BlockSpec((1, bm, cw), lambda b, i: (b, i, 0)),
        out_shape=jax.ShapeDtypeStruct((bsz, seq, cw), BF16),
        scratch_shapes=[pltpu.VMEM((halo + bm, cw), F32),
                        pltpu.VMEM((SUBLANES - 1, halo + bm - SUBLANES, cw), F32)],
        compiler_params=pltpu.CompilerParams(
            dimension_semantics=("arbitrary", "arbitrary"), vmem_limit_bytes=VMEM_LIMIT),
        name="conv_branch",
    )(p, p, p, p, p, jnp.broadcast_to(conv_w[:, None, :], (CONV_KERNEL, SUBLANES, cw)),
      conv_b.reshape(1, cw), cn_g.reshape(1, cw), cn_b.reshape(1, cw))


def _out_kernel(ac_ref, aa_ref, mc0_ref, mc1_ref, ma0_ref, ma1_ref, x_ref, gate_ref,
                wc_ref, wa_ref, wo_ref, g_ref, o_ref):
    yc = jnp.dot(ac_ref[0], wc_ref[...], preferred_element_type=F32)
    ya = jnp.dot(aa_ref[0], wa_ref[...], preferred_element_type=F32)
    mc = jnp.concatenate([mc0_ref[0], mc1_ref[0]], axis=-1).astype(F32)
    ma = jnp.concatenate([ma0_ref[0], ma1_ref[0]], axis=-1).astype(F32)
    merged = _sigmoid(mc) * yc + _sigmoid(ma) * ya
    out = jnp.dot(merged.astype(BF16), wo_ref[...], preferred_element_type=F32)
    ms = jnp.mean(out * out, axis=-1, keepdims=True)
    y = out * lax.rsqrt(ms + NORM_EPS) * g_ref[...]
    o_ref[0] = x_ref[0] + gate_ref[0] * y


def _out_call(p, a_conv, a_attn, x, gate, wc, wa, wo, g_post):
    bsz, seq, d = x.shape
    cw = d // 2
    bm = 256
    resident = lambda shape: pl.BlockSpec(shape, lambda b, i: (0, 0),
                                          pipeline_mode=pl.Buffered(1))
    seg = lambda c: pl.BlockSpec((1, bm, cw), lambda b, i: (b, i, c))
    return pl.pallas_call(
        _out_kernel,
        grid=(bsz, seq // bm),
        in_specs=[
            pl.BlockSpec((1, bm, cw), lambda b, i: (b, i, 0)),
            pl.BlockSpec((1, bm, cw), lambda b, i: (b, i, 0)),
            seg(COL_MC), seg(COL_MC + 1), seg(COL_MA), seg(COL_MA + 1),
            pl.BlockSpec((1, bm, d), lambda b, i: (b, i, 0)),
            pl.BlockSpec((1, 1, d), lambda b, i: (b, 0, 0)),
            resident((cw, d)), resident((cw, d)), resident((d, d)),
            pl.BlockSpec((1, d), lambda b, i: (0, 0)),
        ],
        out_specs=pl.BlockSpec((1, bm, d), lambda b, i: (b, i, 0)),
        out_shape=jax.ShapeDtypeStruct((bsz, seq, d), F32),
        compiler_params=pltpu.CompilerParams(
            dimension_semantics=("arbitrary", "arbitrary"), vmem_limit_bytes=VMEM_LIMIT),
        name="merge_out",
    )(a_conv, a_attn, p, p, p, p, x, gate, wc, wa, wo, g_post.reshape(1, d))


def kernel(x, c, w_ada, b_ada, g_pre, g_post, w_in, conv_w, conv_b, cn_g, cn_b, w_conv_out,
           lam_q1, lam_k1, lam_q2, lam_k2, subln_g, w_attn_out, w_o):
    bsz, seq, d = x.shape
    depth = w_ada.shape[0]
    head_dim = lam_q1.shape[1]
    mod = _ada_call(c, w_ada, b_ada).reshape(depth, bsz, 3, 1, d)
    slopes = jnp.asarray(
        (2.0 ** (-8.0 * np.arange(1, N_HEADS + 1) / N_HEADS)) * LOG2E, dtype=F32)
    for l in range(depth):
        p = _in_call(x, g_pre[l], mod[l, :, 1], mod[l, :, 0], w_in[l].astype(BF16),
                     head_dim=head_dim)
        lam_vecs = jnp.stack([lam_q1[l], lam_k1[l], lam_q2[l], lam_k2[l]])
        a_attn = _attn_call(p, lam_vecs, subln_g[l], slopes, layer_idx=l, d=d)
        a_conv = _conv_call(p, conv_w[l], conv_b[l], cn_g[l], cn_b[l])
        x = _out_call(p, a_conv, a_attn, x, mod[l, :, 2], w_conv_out[l].astype(BF16),
                      w_attn_out[l].astype(BF16), w_o[l].astype(BF16), g_post[l])
    return x
```

```python
import functools
import math

import jax
import jax.numpy as jnp
import numpy as np
from jax import lax
from jax.experimental import pallas as pl
from jax.experimental.pallas import tpu as pltpu

F32 = jnp.float32
BF16 = jnp.bfloat16

NORM_EPS = 1e-6
CONV_KERNEL = 31
N_HEADS = 8
LOG2E = math.log2(math.e)

LANES = 128
SUBLANES = 8
V7X_VMEM_BYTES = 64 * 1024 * 1024
VMEM_LIMIT = V7X_VMEM_BYTES - 8 * 1024 * 1024

COL_CV, COL_CG, COL_CZ, COL_Q, COL_K, COL_V, COL_AZ, COL_MC, COL_MA = 0, 1, 2, 3, 4, 5, 6, 7, 9


def _lambda_init(layer_idx):
    return 0.8 - 0.6 * math.exp(-0.3 * layer_idx)


def _sigmoid(v):
    return 1.0 / (1.0 + jnp.exp(-v))


def _silu(v):
    return v * _sigmoid(v)


def _ada_kernel(c_ref, w_ref, b_ref, o_ref):
    c = c_ref[...]
    ca = _silu(c).astype(BF16)
    o_ref[0] = jnp.dot(ca, w_ref[0].astype(BF16), preferred_element_type=F32) + b_ref[0]


def _ada_call(c, w_ada, b_ada):
    depth, d, d3 = w_ada.shape
    bsz = c.shape[0]
    bn = 1024
    return pl.pallas_call(
        _ada_kernel,
        grid=(depth, d3 // bn),
        in_specs=[
            pl.BlockSpec((bsz, d), lambda l, n: (0, 0)),
            pl.BlockSpec((1, d, bn), lambda l, n: (l, 0, n)),
            pl.BlockSpec((1, 1, bn), lambda l, n: (l, 0, n)),
        ],
        out_specs=pl.BlockSpec((1, bsz, bn), lambda l, n: (l, 0, n)),
        out_shape=jax.ShapeDtypeStruct((depth, bsz, d3), F32),
        compiler_params=pltpu.CompilerParams(
            dimension_semantics=("arbitrary", "arbitrary"), vmem_limit_bytes=VMEM_LIMIT),
        name="ada_mod",
    )(c, w_ada, b_ada.reshape(depth, 1, d3))


def _modulated_norm(xs, gm, sh):
    ms = jnp.mean(xs * xs, axis=-1, keepdims=True)
    return ((xs * lax.rsqrt(ms + NORM_EPS)) * gm + sh).astype(BF16)


def _norm_kernel(x_ref, g_ref, sc_ref, sh_ref, o_ref, *, bm, rows):
    gm = g_ref[...] * (1.0 + sc_ref[0])
    sh = sh_ref[0]

    def body(r, carry):
        r0 = pl.multiple_of(r * rows, rows)
        o_ref[0, pl.ds(r0, rows), :] = _modulated_norm(x_ref[0, pl.ds(r0, rows), :], gm, sh)
        return carry

    lax.fori_loop(0, bm // rows, body, 0)


def _norm_call(x, g_pre, scale, shift):
    bsz, seq, d = x.shape
    bm = 512
    vec = pl.BlockSpec((1, 1, d), lambda b, m: (b, 0, 0))
    return pl.pallas_call(
        functools.partial(_norm_kernel, bm=bm, rows=16),
        grid=(bsz, seq // bm),
        in_specs=[pl.BlockSpec((1, bm, d), lambda b, m: (b, m, 0)),
                  pl.BlockSpec((1, d), lambda b, m: (0, 0)), vec, vec],
        out_specs=pl.BlockSpec((1, bm, d), lambda b, m: (b, m, 0)),
        out_shape=jax.ShapeDtypeStruct((bsz, seq, d), BF16),
        compiler_params=pltpu.CompilerParams(
            dimension_semantics=("arbitrary", "arbitrary"), vmem_limit_bytes=VMEM_LIMIT),
        name="pre_norm",
    )(x, g_pre.reshape(1, d), scale, shift)


def _in_kernel(h_ref, w_ref, o_ref, *, q_tile, q_mult):
    acc = jnp.dot(h_ref[0], w_ref[...], preferred_element_type=F32)
    mult = jnp.where(pl.program_id(2) == q_tile, jnp.float32(q_mult), jnp.float32(1.0))
    o_ref[0] = (acc * mult).astype(BF16)


def _in_call(h, w_in_bf16, *, head_dim):
    bsz, seq, d = h.shape
    in_width = w_in_bf16.shape[1]
    bm, bn = 2048, d // 2
    return pl.pallas_call(
        functools.partial(_in_kernel, q_tile=COL_Q, q_mult=(head_dim ** -0.5) * LOG2E),
        grid=(bsz, seq // bm, in_width // bn),
        in_specs=[
            pl.BlockSpec((1, bm, d), lambda b, m, n: (b, m, 0)),
            pl.BlockSpec((d, bn), lambda b, m, n: (0, n)),
        ],
        out_specs=pl.BlockSpec((1, bm, bn), lambda b, m, n: (b, m, n)),
        out_shape=jax.ShapeDtypeStruct((bsz, seq, in_width), BF16),
        compiler_params=pltpu.CompilerParams(
            dimension_semantics=("arbitrary", "arbitrary", "arbitrary"),
            vmem_limit_bytes=VMEM_LIMIT),
        name="in_proj",
    )(h, w_in_bf16)


ONES_ROWS = 16
N_POS = 9


def _split3_bf16(v):
    p1 = v.astype(BF16).astype(F32)
    p2 = (v - p1).astype(BF16).astype(F32)
    p3 = (v - p1 - p2).astype(BF16).astype(F32)
    return p1, p2, p3


def _attn_kernel(slope_ref, q_ref, k_ref, v_ref, az_ref, lam_ref, g_ref, o_ref,
                 vt_scr, pos_scr, q1_scr, q2_scr, acc1_scr, acc2_scr,
                 sa1_scr, sa2_scr, sb1_scr, sb2_scr, p1_scr, p2_scr, *, bq, bk, half, lam0):
    h = pl.program_id(1)
    i = pl.program_id(2)
    slope = slope_ref[h]
    vd = 2 * half
    nkv = vt_scr.shape[0]
    per_q = bq // bk

    @pl.when(i == 0)
    def _():
        for j in range(nkv):
            vt_scr[j, 0:vd, :] = v_ref[0, j * bk:(j + 1) * bk, :].T
            vt_scr[j, vd:, :] = jnp.ones((ONES_ROWS, bk), BF16)
        ki = lax.broadcasted_iota(jnp.int32, (bk, vd), 0)
        col = lax.broadcasted_iota(jnp.int32, (bk, vd), 1)
        lo = ki & 1
        feat = jnp.where(col < 3, ki - lo, jnp.where(col < 6, lo, jnp.where(col < N_POS, 1, 0)))
        pos_scr[...] = feat.astype(F32).astype(BF16)
        cvec = jnp.full((1, bq), slope, F32)
        c1, c2, c3 = _split3_bf16(cvec)
        qi = lax.broadcasted_iota(jnp.int32, (1, bq), 1).astype(F32)
        r1, r2, r3 = _split3_bf16(-(cvec * qi))
        row = lax.broadcasted_iota(jnp.int32, (vd, bq), 0)
        rows = jnp.zeros((vd, bq), F32)
        for r, val in enumerate((c1, c2, c3, c1, c2, c3, r1, r2, r3)):
            rows = jnp.where(row == r, val, rows)
        q1_scr[vd:, :] = rows.astype(BF16)
        q2_scr[vd:, :] = rows.astype(BF16)

    qt = q_ref[0].astype(F32).T
    row = lax.broadcasted_iota(jnp.int32, qt.shape, 0)
    q1_scr[0:vd, :] = jnp.where(row < half, qt, 0.0).astype(BF16)
    q2_scr[0:vd, :] = jnp.where(row >= half, qt, 0.0).astype(BF16)
    acc1_scr[...] = jnp.zeros_like(acc1_scr)
    acc2_scr[...] = jnp.zeros_like(acc2_scr)
    p1_scr[...] = jnp.zeros_like(p1_scr)
    p2_scr[...] = jnp.zeros_like(p2_scr)

    def scores(j, q_scr, s_ref):
        k = k_ref[0, pl.ds(pl.multiple_of(j * bk, bk), bk), :]
        ka = jnp.concatenate([k, pos_scr[...]], axis=1)
        s = jnp.dot(ka, q_scr[...], preferred_element_type=F32)
        s_ref[...] = s
        return jnp.max(s, axis=0, keepdims=True)

    def softmax(s, smax, off, m_old):
        m_new = jnp.maximum(m_old, smax + off)
        alpha = jnp.exp2(m_old - m_new)
        p = jnp.exp2(s + (off - m_new)).astype(BF16)
        return m_new, alpha, p

    def flush(j):
        vt = vt_scr[jnp.maximum(j, 0)]
        acc1_scr[...] += jnp.dot(vt, p1_scr[...], preferred_element_type=F32)
        acc2_scr[...] += jnp.dot(vt, p2_scr[...], preferred_element_type=F32)

    def step(j, cur, nxt, carry, keep=None):
        m1, m2, smax1, smax2 = carry
        off = slope * (j * bk - i * bq).astype(F32)
        flush(j - 1)
        if nxt is not None:
            next1 = scores(j + 1, q1_scr, nxt[0])
            next2 = scores(j + 1, q2_scr, nxt[1])
        else:
            next1 = next2 = None
        s1, s2 = cur[0][...], cur[1][...]
        if keep is not None:
            s1, s2 = jnp.where(keep, s1, -jnp.inf), jnp.where(keep, s2, -jnp.inf)
            smax1, smax2 = jnp.max(s1, axis=0, keepdims=True), jnp.max(s2, axis=0, keepdims=True)
        m1, alpha1, p1 = softmax(s1, smax1, off, m1)
        acc1_scr[...] = alpha1 * acc1_scr[...]
        p1_scr[...] = p1
        m2, alpha2, p2 = softmax(s2, smax2, off, m2)
        acc2_scr[...] = alpha2 * acc2_scr[...]
        p2_scr[...] = p2
        return m1, m2, next1, next2

    assert per_q == 2
    buf_a, buf_b = (sa1_scr, sa2_scr), (sb1_scr, sb2_scr)
    neg = jnp.full((1, bq), -jnp.inf, F32)
    first1 = scores(0, q1_scr, sa1_scr)
    first2 = scores(0, q2_scr, sa2_scr)

    def pair(t, carry):
        a = 2 * t
        carry = step(a, buf_a, buf_b, carry)
        return step(a + 1, buf_b, buf_a, carry)

    carry = lax.fori_loop(0, i, pair, (neg, neg, first1, first2))

    ki = lax.broadcasted_iota(jnp.int32, (bk, bq), 0)
    qi = lax.broadcasted_iota(jnp.int32, (bk, bq), 1)
    d0 = per_q * i
    carry = step(d0, buf_a, buf_b, carry, keep=ki <= qi)
    step(d0 + 1, buf_b, None, carry, keep=ki + bk <= qi)
    flush(d0 + 1)

    lq1, lk1, lq2, lk2 = lam_ref[0:1, :], lam_ref[1:2, :], lam_ref[2:3, :], lam_ref[3:4, :]
    lam = (jnp.exp(jnp.sum(lq1 * lk1, axis=-1, keepdims=True))
           - jnp.exp(jnp.sum(lq2 * lk2, axis=-1, keepdims=True)) + lam0)
    inv1 = 1.0 / acc1_scr[vd:vd + 1, :]
    inv2 = 1.0 / acc2_scr[vd:vd + 1, :]
    ot = acc1_scr[0:vd, :] * inv1 - lam * (acc2_scr[0:vd, :] * inv2)
    ms = jnp.mean(ot * ot, axis=0, keepdims=True)
    ot = ot * lax.rsqrt(ms + NORM_EPS) * g_ref[...] * (1.0 - lam0)
    az = az_ref[0].astype(F32)
    o_ref[0] = (ot.T * _silu(az)).astype(BF16)


def _attn_call(p, lam_vecs, subln_g, slopes, *, layer_idx, d):
    bsz, seq, _ = p.shape
    bq, bk = 1024, 512
    nkv = seq // bk
    v_dim = d // (2 * N_HEADS)
    assert v_dim == LANES
    cw = d // 2
    col = lambda c: c * cw // v_dim
    return pl.pallas_call(
        functools.partial(_attn_kernel, bq=bq, bk=bk, half=v_dim // 2,
                          lam0=_lambda_init(layer_idx)),
        grid=(bsz, N_HEADS, seq // bq),
        in_specs=[
            pl.BlockSpec(memory_space=pltpu.SMEM),
            pl.BlockSpec((1, bq, v_dim), lambda b, h, i: (b, i, col(COL_Q) + h)),
            pl.BlockSpec((1, seq, v_dim), lambda b, h, i: (b, 0, col(COL_K) + h)),
            pl.BlockSpec((1, seq, v_dim), lambda b, h, i: (b, 0, col(COL_V) + h)),
            pl.BlockSpec((1, bq, v_dim), lambda b, h, i: (b, i, col(COL_AZ) + h)),
            pl.BlockSpec((4, v_dim // 2), lambda b, h, i: (0, 0)),
            pl.BlockSpec((v_dim, 1), lambda b, h, i: (0, 0)),
        ],
        out_specs=pl.BlockSpec((1, bq, v_dim), lambda b, h, i: (b, i, h)),
        out_shape=jax.ShapeDtypeStruct((bsz, seq, N_HEADS * v_dim), BF16),
        scratch_shapes=[
            pltpu.VMEM((nkv, v_dim + ONES_ROWS, bk), BF16),
            pltpu.VMEM((bk, v_dim), BF16),
            pltpu.VMEM((2 * v_dim, bq), BF16),
            pltpu.VMEM((2 * v_dim, bq), BF16),
            pltpu.VMEM((v_dim + ONES_ROWS, bq), F32),
            pltpu.VMEM((v_dim + ONES_ROWS, bq), F32),
            pltpu.VMEM((bk, bq), F32), pltpu.VMEM((bk, bq), F32),
            pltpu.VMEM((bk, bq), F32), pltpu.VMEM((bk, bq), F32),
            pltpu.VMEM((bk, bq), BF16), pltpu.VMEM((bk, bq), BF16),
        ],
        compiler_params=pltpu.CompilerParams(
            dimension_semantics=("arbitrary", "arbitrary", "arbitrary"),
            vmem_limit_bytes=VMEM_LIMIT),
        name="diff_attn",
    )(slopes, p, p, p, p, lam_vecs, subln_g.reshape(v_dim, 1))


def _conv_kernel(cv_ref, cg_ref, cvh_ref, cgh_ref, cz_ref, w_ref, b_ref, g_ref, be_ref, o_ref,
                 u_scr, sh_scr, *, bm, halo, rows):
    i = pl.program_id(1)
    uh = cvh_ref[0].astype(F32) * _sigmoid(cgh_ref[0].astype(F32))
    u_scr[0:halo, :] = jnp.where(i > 0, uh, 0.0)
    u_scr[halo:halo + bm, :] = cv_ref[0].astype(F32) * _sigmoid(cg_ref[0].astype(F32))
    n_sh = sh_scr.shape[1]
    u_all = u_scr[...]
    for b in range(1, SUBLANES):
        sh_scr[b - 1] = pltpu.roll(u_all, halo + bm - b, 0)[0:n_sh, :]
    first = halo - (CONV_KERNEL - 1)
    cw = u_scr.shape[1]
    for r0 in range(0, bm, rows):
        acc = jnp.zeros((rows // SUBLANES, SUBLANES, cw), F32)
        for t in range(CONV_KERNEL):
            b = (first + t) % SUBLANES
            a0 = r0 + first + t - b
            tap = u_scr[a0:a0 + rows, :] if b == 0 else sh_scr[b - 1, a0:a0 + rows, :]
            acc = acc + tap.reshape(acc.shape) * w_ref[t]
        acc = acc.reshape(rows, cw) + b_ref[...]
        mu = jnp.mean(acc, axis=-1, keepdims=True)
        cen = acc - mu
        var = jnp.mean(cen * cen, axis=-1, keepdims=True)
        y = cen * lax.rsqrt(var + NORM_EPS) * g_ref[...] + be_ref[...]
        cz = cz_ref[0, r0:r0 + rows, :].astype(F32)
        o_ref[0, r0:r0 + rows, :] = (_silu(y) * _silu(cz)).astype(BF16)


def _conv_call(p, conv_w, conv_b, cn_g, cn_b):
    bsz, seq, _ = p.shape
    cw = conv_w.shape[1]
    bm, halo = 256, 32
    assert halo >= CONV_KERNEL - 1 and bm % halo == 0
    hb = bm // halo
    cur = lambda c: pl.BlockSpec((1, bm, cw), lambda b, i: (b, i, c))
    prev = lambda c: pl.BlockSpec((1, halo, cw), lambda b, i: (b, jnp.maximum(i * hb - 1, 0), c))
    vec = pl.BlockSpec((1, cw), lambda b, i: (0, 0))
    return pl.pallas_call(
        functools.partial(_conv_kernel, bm=bm, halo=halo, rows=32),
        grid=(bsz, seq // bm),
        in_specs=[cur(COL_CV), cur(COL_CG), prev(COL_CV), prev(COL_CG), cur(COL_CZ),
                  pl.BlockSpec((CONV_KERNEL, SUBLANES, cw), lambda b, i: (0, 0, 0)), vec, vec, vec],
        out_specs=pl.BlockSpec((1, bm, cw), lambda b, i: (b, i, 0)),
        out_shape=jax.ShapeDtypeStruct((bsz, seq, cw), BF16),
        scratch_shapes=[pltpu.VMEM((halo + bm, cw), F32),
                        pltpu.VMEM((SUBLANES - 1, halo + bm - SUBLANES, cw), F32)],
        compiler_params=pltpu.CompilerParams(
            dimension_semantics=("arbitrary", "arbitrary"), vmem_limit_bytes=VMEM_LIMIT),
        name="conv_branch",
    )(p, p, p, p, p, jnp.broadcast_to(conv_w[:, None, :], (CONV_KERNEL, SUBLANES, cw)),
      conv_b.reshape(1, cw), cn_g.reshape(1, cw), cn_b.reshape(1, cw))


def _out_kernel(ac_ref, aa_ref, mc0_ref, mc1_ref, ma0_ref, ma1_ref, x_ref, gate_ref,
                wc_ref, wa_ref, wo_ref, g_ref, *rest, emit_h):
    yc = jnp.dot(ac_ref[0], wc_ref[...], preferred_element_type=F32)
    ya = jnp.dot(aa_ref[0], wa_ref[...], preferred_element_type=F32)
    mc = jnp.concatenate([mc0_ref[0], mc1_ref[0]], axis=-1).astype(F32)
    ma = jnp.concatenate([ma0_ref[0], ma1_ref[0]], axis=-1).astype(F32)
    merged = _sigmoid(mc) * yc + _sigmoid(ma) * ya
    out = jnp.dot(merged.astype(BF16), wo_ref[...], preferred_element_type=F32)
    ms = jnp.mean(out * out, axis=-1, keepdims=True)
    y = out * lax.rsqrt(ms + NORM_EPS) * g_ref[...]
    x_new = x_ref[0] + gate_ref[0] * y
    if emit_h:
        gn_ref, scn_ref, shn_ref, o_ref, h_ref = rest
        h_ref[0] = _modulated_norm(x_new, gn_ref[...] * (1.0 + scn_ref[0]), shn_ref[0])
    else:
        (o_ref,) = rest
    o_ref[0] = x_new


def _out_call(p, a_conv, a_attn, x, gate, wc, wa, wo, g_post, next_norm=None):
    bsz, seq, d = x.shape
    cw = d // 2
    bm = 256
    resident = lambda shape: pl.BlockSpec(shape, lambda b, i: (0, 0),
                                          pipeline_mode=pl.Buffered(1))
    seg = lambda c: pl.BlockSpec((1, bm, cw), lambda b, i: (b, i, c))
    rows = pl.BlockSpec((1, bm, d), lambda b, i: (b, i, 0))
    per_batch = pl.BlockSpec((1, 1, d), lambda b, i: (b, 0, 0))
    shared = pl.BlockSpec((1, d), lambda b, i: (0, 0))
    in_specs = [
        pl.BlockSpec((1, bm, cw), lambda b, i: (b, i, 0)),
        pl.BlockSpec((1, bm, cw), lambda b, i: (b, i, 0)),
        seg(COL_MC), seg(COL_MC + 1), seg(COL_MA), seg(COL_MA + 1),
        rows, per_batch,
        resident((cw, d)), resident((cw, d)), resident((d, d)),
        shared,
    ]
    args = [a_conv, a_attn, p, p, p, p, x, gate, wc, wa, wo, g_post.reshape(1, d)]
    out_specs, out_shape = rows, jax.ShapeDtypeStruct((bsz, seq, d), F32)
    if next_norm is not None:
        g_next, scale_next, shift_next = next_norm
        in_specs += [shared, per_batch, per_batch]
        args += [g_next.reshape(1, d), scale_next, shift_next]
        out_specs = (rows, rows)
        out_shape = (out_shape, jax.ShapeDtypeStruct((bsz, seq, d), BF16))
    return pl.pallas_call(
        functools.partial(_out_kernel, emit_h=next_norm is not None),
        grid=(bsz, seq // bm),
        in_specs=in_specs,
        out_specs=out_specs,
        out_shape=out_shape,
        compiler_params=pltpu.CompilerParams(
            dimension_semantics=("arbitrary", "arbitrary"), vmem_limit_bytes=VMEM_LIMIT),
        name="merge_out",
    )(*args)


def kernel(x, c, w_ada, b_ada, g_pre, g_post, w_in, conv_w, conv_b, cn_g, cn_b, w_conv_out,
           lam_q1, lam_k1, lam_q2, lam_k2, subln_g, w_attn_out, w_o):
    bsz, seq, d = x.shape
    depth = w_ada.shape[0]
    head_dim = lam_q1.shape[1]
    mod = _ada_call(c, w_ada, b_ada).reshape(depth, bsz, 3, 1, d)
    slopes = jnp.asarray(
        (2.0 ** (-8.0 * np.arange(1, N_HEADS + 1) / N_HEADS)) * LOG2E, dtype=F32)
    h = _norm_call(x, g_pre[0], mod[0, :, 1], mod[0, :, 0])
    for l in range(depth):
        p = _in_call(h, w_in[l].astype(BF16), head_dim=head_dim)
        lam_vecs = jnp.stack([lam_q1[l], lam_k1[l], lam_q2[l], lam_k2[l]])
        a_attn = _attn_call(p, lam_vecs, subln_g[l], slopes, layer_idx=l, d=d)
        a_conv = _conv_call(p, conv_w[l], conv_b[l], cn_g[l], cn_b[l])
        nxt = (g_pre[l + 1], mod[l + 1, :, 1], mod[l + 1, :, 0]) if l + 1 < depth else None
        res = _out_call(p, a_conv, a_attn, x, mod[l, :, 2], w_conv_out[l].astype(BF16),
                        w_attn_out[l].astype(BF16), w_o[l].astype(BF16), g_post[l], nxt)
        x, h = res if nxt is not None else (res, None)
    return x
```

```python
import functools
import math

import jax
import jax.numpy as jnp
import numpy as np
from jax import lax
from jax.experimental import pallas as pl
from jax.experimental.pallas import tpu as pltpu

F32 = jnp.float32
BF16 = jnp.bfloat16

NORM_EPS = 1e-6
CONV_KERNEL = 31
N_HEADS = 8
LOG2E = math.log2(math.e)

LANES = 128
SUBLANES = 8
V7X_VMEM_BYTES = 64 * 1024 * 1024
VMEM_LIMIT = V7X_VMEM_BYTES - 8 * 1024 * 1024

COL_CV, COL_CG, COL_CZ, COL_Q, COL_K, COL_V, COL_AZ, COL_MC, COL_MA = 0, 1, 2, 3, 4, 5, 6, 7, 9


def _lambda_init(layer_idx):
    return 0.8 - 0.6 * math.exp(-0.3 * layer_idx)


def _sigmoid(v):
    return 1.0 / (1.0 + jnp.exp(-v))


def _silu(v):
    return v * _sigmoid(v)


def _ada_kernel(c_ref, w_ref, b_ref, o_ref):
    c = c_ref[...]
    ca = _silu(c).astype(BF16)
    o_ref[0] = jnp.dot(ca, w_ref[0].astype(BF16), preferred_element_type=F32) + b_ref[0]


def _ada_call(c, w_ada, b_ada):
    depth, d, d3 = w_ada.shape
    bsz = c.shape[0]
    bn = 1024
    return pl.pallas_call(
        _ada_kernel,
        grid=(depth, d3 // bn),
        in_specs=[
            pl.BlockSpec((bsz, d), lambda l, n: (0, 0)),
            pl.BlockSpec((1, d, bn), lambda l, n: (l, 0, n)),
            pl.BlockSpec((1, 1, bn), lambda l, n: (l, 0, n)),
        ],
        out_specs=pl.BlockSpec((1, bsz, bn), lambda l, n: (l, 0, n)),
        out_shape=jax.ShapeDtypeStruct((depth, bsz, d3), F32),
        compiler_params=pltpu.CompilerParams(
            dimension_semantics=("arbitrary", "arbitrary"), vmem_limit_bytes=VMEM_LIMIT),
        name="ada_mod",
    )(c, w_ada, b_ada.reshape(depth, 1, d3))


def _modulated_norm(xs, gm, sh):
    ms = jnp.mean(xs * xs, axis=-1, keepdims=True)
    return ((xs * lax.rsqrt(ms + NORM_EPS)) * gm + sh).astype(BF16)


def _norm_kernel(x_ref, g_ref, sc_ref, sh_ref, o_ref, *, bm, rows):
    gm = g_ref[...] * (1.0 + sc_ref[0])
    sh = sh_ref[0]

    def body(r, carry):
        r0 = pl.multiple_of(r * rows, rows)
        o_ref[0, pl.ds(r0, rows), :] = _modulated_norm(x_ref[0, pl.ds(r0, rows), :], gm, sh)
        return carry

    lax.fori_loop(0, bm // rows, body, 0, unroll=4)


def _norm_call(x, g_pre, scale, shift):
    bsz, seq, d = x.shape
    bm = 512
    vec = pl.BlockSpec((1, 1, d), lambda b, m: (b, 0, 0))
    return pl.pallas_call(
        functools.partial(_norm_kernel, bm=bm, rows=16),
        grid=(bsz, seq // bm),
        in_specs=[pl.BlockSpec((1, bm, d), lambda b, m: (b, m, 0)),
                  pl.BlockSpec((1, d), lambda b, m: (0, 0)), vec, vec],
        out_specs=pl.BlockSpec((1, bm, d), lambda b, m: (b, m, 0)),
        out_shape=jax.ShapeDtypeStruct((bsz, seq, d), BF16),
        compiler_params=pltpu.CompilerParams(
            dimension_semantics=("arbitrary", "arbitrary"), vmem_limit_bytes=VMEM_LIMIT),
        name="pre_norm",
    )(x, g_pre.reshape(1, d), scale, shift)


def _in_kernel(h_ref, w_ref, o_ref, *, q_tile, q_mult):
    acc = jnp.dot(h_ref[0], w_ref[...], preferred_element_type=F32)
    mult = jnp.where(pl.program_id(2) == q_tile, jnp.float32(q_mult), jnp.float32(1.0))
    o_ref[0] = (acc * mult).astype(BF16)


def _in_call(h, w_in_bf16, *, head_dim):
    bsz, seq, d = h.shape
    in_width = w_in_bf16.shape[1]
    bm, bn = 2048, d // 2
    return pl.pallas_call(
        functools.partial(_in_kernel, q_tile=COL_Q, q_mult=(head_dim ** -0.5) * LOG2E),
        grid=(bsz, seq // bm, in_width // bn),
        in_specs=[
            pl.BlockSpec((1, bm, d), lambda b, m, n: (b, m, 0)),
            pl.BlockSpec((d, bn), lambda b, m, n: (0, n)),
        ],
        out_specs=pl.BlockSpec((1, bm, bn), lambda b, m, n: (b, m, n)),
        out_shape=jax.ShapeDtypeStruct((bsz, seq, in_width), BF16),
        compiler_params=pltpu.CompilerParams(
            dimension_semantics=("arbitrary", "arbitrary", "arbitrary"),
            vmem_limit_bytes=VMEM_LIMIT),
        name="in_proj",
    )(h, w_in_bf16)


ONES_ROWS = 16
N_POS = 9


def _split3_bf16(v):
    p1 = v.astype(BF16).astype(F32)
    p2 = (v - p1).astype(BF16).astype(F32)
    p3 = (v - p1 - p2).astype(BF16).astype(F32)
    return p1, p2, p3


def _attn_kernel(slope_ref, q_ref, k_ref, v_ref, az_ref, lam_ref, g_ref, o_ref,
                 vt_scr, pos_scr, q1_scr, q2_scr, acc1_scr, acc2_scr,
                 sa1_scr, sa2_scr, sb1_scr, sb2_scr, p1_scr, p2_scr, m_scr, *, bq, bk, half, lam0):
    h = pl.program_id(1)
    i = pl.program_id(2)
    slope = slope_ref[h]
    vd = 2 * half
    nkv = vt_scr.shape[0]
    per_q = bq // bk

    @pl.when(i == 0)
    def _():
        for j in range(nkv):
            vt_scr[j, 0:vd, :] = v_ref[0, j * bk:(j + 1) * bk, :].T
            vt_scr[j, vd:, :] = jnp.ones((ONES_ROWS, bk), BF16)
        ki = lax.broadcasted_iota(jnp.int32, (bk, vd), 0)
        col = lax.broadcasted_iota(jnp.int32, (bk, vd), 1)
        lo = ki & 1
        feat = jnp.where(col < 3, ki - lo, jnp.where(col < 6, lo, jnp.where(col < N_POS, 1, 0)))
        pos_scr[...] = feat.astype(F32).astype(BF16)
        cvec = jnp.full((1, bq), slope, F32)
        c1, c2, c3 = _split3_bf16(cvec)
        qi = lax.broadcasted_iota(jnp.int32, (1, bq), 1).astype(F32)
        r1, r2, r3 = _split3_bf16(-(cvec * qi))
        row = lax.broadcasted_iota(jnp.int32, (vd, bq), 0)
        rows = jnp.zeros((vd, bq), F32)
        for r, val in enumerate((c1, c2, c3, c1, c2, c3, r1, r2, r3)):
            rows = jnp.where(row == r, val, rows)
        q1_scr[vd:, :] = rows.astype(BF16)
        q2_scr[vd:, :] = rows.astype(BF16)
        acc1_scr[...] = jnp.zeros_like(acc1_scr)
        acc2_scr[...] = jnp.zeros_like(acc2_scr)
        p1_scr[...] = jnp.zeros_like(p1_scr)
        p2_scr[...] = jnp.zeros_like(p2_scr)

    qt = q_ref[0].astype(F32).T
    row = lax.broadcasted_iota(jnp.int32, qt.shape, 0)
    q1_scr[0:vd, :] = jnp.where(row < half, qt, 0.0).astype(BF16)
    q2_scr[0:vd, :] = jnp.where(row >= half, qt, 0.0).astype(BF16)

    def scores(j, q_scr, s_ref, lanes=slice(None)):
        k = k_ref[0, pl.ds(pl.multiple_of(j * bk, bk), bk), :]
        ka = jnp.concatenate([k, pos_scr[...]], axis=1)
        s = jnp.dot(ka, q_scr[:, lanes], preferred_element_type=F32)
        s_ref[:, lanes] = s
        return jnp.max(s, axis=0, keepdims=True)

    def softmax(s, smax, off, m_old):
        m_new = jnp.maximum(m_old, smax + off)
        alpha = jnp.exp2(m_old - m_new)
        p = jnp.exp2(s + (off - m_new)).astype(BF16)
        return m_new, alpha, p

    def flush(j, lanes=slice(None)):
        vt = vt_scr[jnp.maximum(j, 0)]
        acc1_scr[:, lanes] += jnp.dot(vt, p1_scr[:, lanes], preferred_element_type=F32)
        acc2_scr[:, lanes] += jnp.dot(vt, p2_scr[:, lanes], preferred_element_type=F32)

    def step(j, cur, nxt, carry, keep=None, next_lanes=slice(None)):
        m1, m2, smax1, smax2 = carry
        off = slope * (j * bk - i * bq).astype(F32)
        flush(j - 1)
        next1 = scores(j + 1, q1_scr, nxt[0], next_lanes)
        next2 = scores(j + 1, q2_scr, nxt[1], next_lanes)
        s1, s2 = cur[0][...], cur[1][...]
        if keep is not None:
            s1, s2 = jnp.where(keep, s1, -jnp.inf), jnp.where(keep, s2, -jnp.inf)
            smax1, smax2 = jnp.max(s1, axis=0, keepdims=True), jnp.max(s2, axis=0, keepdims=True)
        m1, alpha1, p1 = softmax(s1, smax1, off, m1)
        acc1_scr[...] = alpha1 * acc1_scr[...]
        p1_scr[...] = p1
        m2, alpha2, p2 = softmax(s2, smax2, off, m2)
        acc2_scr[...] = alpha2 * acc2_scr[...]
        p2_scr[...] = p2
        return m1, m2, next1, next2

    assert per_q == 2
    buf_a, buf_b = (sa1_scr, sa2_scr), (sb1_scr, sb2_scr)
    neg = jnp.full((1, bq), -jnp.inf, F32)
    first1 = scores(0, q1_scr, sa1_scr)
    first2 = scores(0, q2_scr, sa2_scr)

    def pair(a, carry):
        carry = step(a, buf_a, buf_b, carry)
        return step(a + 1, buf_b, buf_a, carry)

    carry = lax.fori_loop(0, i // 2, lambda t, c: pair(4 * t + 2, pair(4 * t, c)),
                          (neg, neg, first1, first2))
    carry = lax.cond(i % 2 == 1, lambda c: pair(per_q * (i - 1), c), lambda c: c, carry)

    ki = lax.broadcasted_iota(jnp.int32, (bk, bq), 0)
    qi = lax.broadcasted_iota(jnp.int32, (bk, bq), 1)
    d0 = per_q * i
    hi = slice(bk, bq)
    m1, m2, _, _ = step(d0, buf_a, buf_b, carry, keep=ki <= qi, next_lanes=hi)
    off = slope * ((d0 + 1) * bk - i * bq).astype(F32)
    flush(d0)
    m_scr[0:SUBLANES, :] = jnp.broadcast_to(m1, (SUBLANES, bq))
    m_scr[SUBLANES:, :] = jnp.broadcast_to(m2, (SUBLANES, bq))
    keep_hi = (lax.broadcasted_iota(jnp.int32, (bk, bk), 0)
               <= lax.broadcasted_iota(jnp.int32, (bk, bk), 1))
    for n, (s_ref, acc_scr, p_scr) in enumerate(((sb1_scr, acc1_scr, p1_scr),
                                                 (sb2_scr, acc2_scr, p2_scr))):
        m_old = m_scr[n * SUBLANES:n * SUBLANES + 1, hi]
        s_hi = jnp.where(keep_hi, s_ref[:, hi], -jnp.inf)
        _, alpha, p = softmax(s_hi, jnp.max(s_hi, axis=0, keepdims=True), off, m_old)
        acc_scr[:, hi] = alpha * acc_scr[:, hi]
        p_scr[:, hi] = p
    flush(d0 + 1, hi)

    lq1, lk1, lq2, lk2 = lam_ref[0:1, :], lam_ref[1:2, :], lam_ref[2:3, :], lam_ref[3:4, :]
    lam = (jnp.exp(jnp.sum(lq1 * lk1, axis=-1, keepdims=True))
           - jnp.exp(jnp.sum(lq2 * lk2, axis=-1, keepdims=True)) + lam0)
    inv1 = 1.0 / acc1_scr[vd:vd + 1, :]
    inv2 = 1.0 / acc2_scr[vd:vd + 1, :]
    ot = acc1_scr[0:vd, :] * inv1 - lam * (acc2_scr[0:vd, :] * inv2)
    ms = jnp.mean(ot * ot, axis=0, keepdims=True)
    ot = ot * lax.rsqrt(ms + NORM_EPS) * g_ref[...] * (1.0 - lam0)
    az = az_ref[0].astype(F32)
    o_ref[0] = (ot.T * _silu(az)).astype(BF16)


def _attn_call(p, lam_vecs, subln_g, slopes, *, layer_idx, d):
    bsz, seq, _ = p.shape
    bq, bk = 1024, 512
    nkv = seq // bk
    v_dim = d // (2 * N_HEADS)
    assert v_dim == LANES
    cw = d // 2
    col = lambda c: c * cw // v_dim
    return pl.pallas_call(
        functools.partial(_attn_kernel, bq=bq, bk=bk, half=v_dim // 2,
                          lam0=_lambda_init(layer_idx)),
        grid=(bsz, N_HEADS, seq // bq),
        in_specs=[
            pl.BlockSpec(memory_space=pltpu.SMEM),
            pl.BlockSpec((1, bq, v_dim), lambda b, h, i: (b, i, col(COL_Q) + h)),
            pl.BlockSpec((1, seq, v_dim), lambda b, h, i: (b, 0, col(COL_K) + h)),
            pl.BlockSpec((1, seq, v_dim), lambda b, h, i: (b, 0, col(COL_V) + h)),
            pl.BlockSpec((1, bq, v_dim), lambda b, h, i: (b, i, col(COL_AZ) + h)),
            pl.BlockSpec((4, v_dim // 2), lambda b, h, i: (0, 0)),
            pl.BlockSpec((v_dim, 1), lambda b, h, i: (0, 0)),
        ],
        out_specs=pl.BlockSpec((1, bq, v_dim), lambda b, h, i: (b, i, h)),
        out_shape=jax.ShapeDtypeStruct((bsz, seq, N_HEADS * v_dim), BF16),
        scratch_shapes=[
            pltpu.VMEM((nkv, v_dim + ONES_ROWS, bk), BF16),
            pltpu.VMEM((bk, v_dim), BF16),
            pltpu.VMEM((2 * v_dim, bq), BF16),
            pltpu.VMEM((2 * v_dim, bq), BF16),
            pltpu.VMEM((v_dim + ONES_ROWS, bq), F32),
            pltpu.VMEM((v_dim + ONES_ROWS, bq), F32),
            pltpu.VMEM((bk, bq), F32), pltpu.VMEM((bk, bq), F32),
            pltpu.VMEM((bk, bq), F32), pltpu.VMEM((bk, bq), F32),
            pltpu.VMEM((bk, bq), BF16), pltpu.VMEM((bk, bq), BF16),
            pltpu.VMEM((2 * SUBLANES, bq), F32),
        ],
        compiler_params=pltpu.CompilerParams(
            dimension_semantics=("arbitrary", "arbitrary", "arbitrary"),
            vmem_limit_bytes=VMEM_LIMIT),
        name="diff_attn",
    )(slopes, p, p, p, p, lam_vecs, subln_g.reshape(v_dim, 1))


def _conv_kernel(cv_ref, cg_ref, cvh_ref, cgh_ref, cz_ref, w_ref, b_ref, g_ref, be_ref, o_ref,
                 u_scr, sh_scr, *, bm, halo, rows):
    i = pl.program_id(1)
    uh = cvh_ref[0].astype(F32) * _sigmoid(cgh_ref[0].astype(F32))
    u_scr[0:halo, :] = jnp.where(i > 0, uh, 0.0)
    u_scr[halo:halo + bm, :] = cv_ref[0].astype(F32) * _sigmoid(cg_ref[0].astype(F32))
    n_sh = sh_scr.shape[1]
    u_all = u_scr[...]
    for b in range(1, SUBLANES):
        sh_scr[b - 1] = pltpu.roll(u_all, halo + bm - b, 0)[0:n_sh, :]
    first = halo - (CONV_KERNEL - 1)
    cw = u_scr.shape[1]
    for r0 in range(0, bm, rows):
        acc = jnp.zeros((rows // SUBLANES, SUBLANES, cw), F32)
        for t in range(CONV_KERNEL):
            b = (first + t) % SUBLANES
            a0 = r0 + first + t - b
            tap = u_scr[a0:a0 + rows, :] if b == 0 else sh_scr[b - 1, a0:a0 + rows, :]
            acc = acc + tap.reshape(acc.shape) * w_ref[t]
        acc = acc.reshape(rows, cw) + b_ref[...]
        mu = jnp.mean(acc, axis=-1, keepdims=True)
        cen = acc - mu
        var = jnp.mean(cen * cen, axis=-1, keepdims=True)
        y = cen * lax.rsqrt(var + NORM_EPS) * g_ref[...] + be_ref[...]
        cz = cz_ref[0, r0:r0 + rows, :].astype(F32)
        o_ref[0, r0:r0 + rows, :] = (_silu(y) * _silu(cz)).astype(BF16)


def _conv_call(p, conv_w, conv_b, cn_g, cn_b):
    bsz, seq, _ = p.shape
    cw = conv_w.shape[1]
    bm, halo = 256, 32
    assert halo >= CONV_KERNEL - 1 and bm % halo == 0
    hb = bm // halo
    cur = lambda c: pl.BlockSpec((1, bm, cw), lambda b, i: (b, i, c))
    prev = lambda c: pl.BlockSpec((1, halo, cw), lambda b, i: (b, jnp.maximum(i * hb - 1, 0), c))
    vec = pl.BlockSpec((1, cw), lambda b, i: (0, 0))
    return pl.pallas_call(
        functools.partial(_conv_kernel, bm=bm, halo=halo, rows=32),
        grid=(bsz, seq // bm),
        in_specs=[cur(COL_CV), cur(COL_CG), prev(COL_CV), prev(COL_CG), cur(COL_CZ),
                  pl.BlockSpec((CONV_KERNEL, SUBLANES, cw), lambda b, i: (0, 0, 0)), vec, vec, vec],
        out_specs=pl.BlockSpec((1, bm, cw), lambda b, i: (b, i, 0)),
        out_shape=jax.ShapeDtypeStruct((bsz, seq, cw), BF16),
        scratch_shapes=[pltpu.VMEM((halo + bm, cw), F32),
                        pltpu.VMEM((SUBLANES - 1, halo + bm - SUBLANES, cw), F32)],
        compiler_params=pltpu.CompilerParams(
            dimension_semantics=("arbitrary", "arbitrary"), vmem_limit_bytes=VMEM_LIMIT),
        name="conv_branch",
    )(p, p, p, p, p, jnp.broadcast_to(conv_w[:, None, :], (CONV_KERNEL, SUBLANES, cw)),
      conv_b.reshape(1, cw), cn_g.reshape(1, cw), cn_b.reshape(1, cw))


def _out_kernel(ac_ref, aa_ref, mc0_ref, mc1_ref, ma0_ref, ma1_ref, x_ref, gate_ref,
                wc_ref, wa_ref, wo_ref, g_ref, *rest, emit_h):
    yc = jnp.dot(ac_ref[0], wc_ref[...], preferred_element_type=F32)
    ya = jnp.dot(aa_ref[0], wa_ref[...], preferred_element_type=F32)
    mc = jnp.concatenate([mc0_ref[0], mc1_ref[0]], axis=-1).astype(F32)
    ma = jnp.concatenate([ma0_ref[0], ma1_ref[0]], axis=-1).astype(F32)
    merged = _sigmoid(mc) * yc + _sigmoid(ma) * ya
    out = jnp.dot(merged.astype(BF16), wo_ref[...], preferred_element_type=F32)
    ms = jnp.mean(out * out, axis=-1, keepdims=True)
    y = out * lax.rsqrt(ms + NORM_EPS) * g_ref[...]
    x_new = x_ref[0] + gate_ref[0] * y
    if emit_h:
        gn_ref, scn_ref, shn_ref, o_ref, h_ref = rest
        h_ref[0] = _modulated_norm(x_new, gn_ref[...] * (1.0 + scn_ref[0]), shn_ref[0])
    else:
        (o_ref,) = rest
    o_ref[0] = x_new


def _out_call(p, a_conv, a_attn, x, gate, wc, wa, wo, g_post, next_norm=None):
    bsz, seq, d = x.shape
    cw = d // 2
    bm = 256
    resident = lambda shape: pl.BlockSpec(shape, lambda b, i: (0, 0),
                                          pipeline_mode=pl.Buffered(1))
    seg = lambda c: pl.BlockSpec((1, bm, cw), lambda b, i: (b, i, c))
    rows = pl.BlockSpec((1, bm, d), lambda b, i: (b, i, 0))
    per_batch = pl.BlockSpec((1, 1, d), lambda b, i: (b, 0, 0))
    shared = pl.BlockSpec((1, d), lambda b, i: (0, 0))
    in_specs = [
        pl.BlockSpec((1, bm, cw), lambda b, i: (b, i, 0)),
        pl.BlockSpec((1, bm, cw), lambda b, i: (b, i, 0)),
        seg(COL_MC), seg(COL_MC + 1), seg(COL_MA), seg(COL_MA + 1),
        rows, per_batch,
        resident((cw, d)), resident((cw, d)), resident((d, d)),
        shared,
    ]
    args = [a_conv, a_attn, p, p, p, p, x, gate, wc, wa, wo, g_post.reshape(1, d)]
    out_specs, out_shape = rows, jax.ShapeDtypeStruct((bsz, seq, d), F32)
    if next_norm is not None:
        g_next, scale_next, shift_next = next_norm
        in_specs += [shared, per_batch, per_batch]
        args += [g_next.reshape(1, d), scale_next, shift_next]
        out_specs = (rows, rows)
        out_shape = (out_shape, jax.ShapeDtypeStruct((bsz, seq, d), BF16))
    return pl.pallas_call(
        functools.partial(_out_kernel, emit_h=next_norm is not None),
        grid=(bsz, seq // bm),
        in_specs=in_specs,
        out_specs=out_specs,
        out_shape=out_shape,
        compiler_params=pltpu.CompilerParams(
            dimension_semantics=("arbitrary", "arbitrary"), vmem_limit_bytes=VMEM_LIMIT),
        name="merge_out",
    )(*args)


def kernel(x, c, w_ada, b_ada, g_pre, g_post, w_in, conv_w, conv_b, cn_g, cn_b, w_conv_out,
           lam_q1, lam_k1, lam_q2, lam_k2, subln_g, w_attn_out, w_o):
    bsz, seq, d = x.shape
    depth = w_ada.shape[0]
    head_dim = lam_q1.shape[1]
    mod = _ada_call(c, w_ada, b_ada).reshape(depth, bsz, 3, 1, d)
    slopes = jnp.asarray(
        (2.0 ** (-8.0 * np.arange(1, N_HEADS + 1) / N_HEADS)) * LOG2E, dtype=F32)
    h = _norm_call(x, g_pre[0], mod[0, :, 1], mod[0, :, 0])
    for l in range(depth):
        p = _in_call(h, w_in[l].astype(BF16), head_dim=head_dim)
        lam_vecs = jnp.stack([lam_q1[l], lam_k1[l], lam_q2[l], lam_k2[l]])
        a_attn = _attn_call(p, lam_vecs, subln_g[l], slopes, layer_idx=l, d=d)
        a_conv = _conv_call(p, conv_w[l], conv_b[l], cn_g[l], cn_b[l])
        nxt = (g_pre[l + 1], mod[l + 1, :, 1], mod[l + 1, :, 0]) if l + 1 < depth else None
        res = _out_call(p, a_conv, a_attn, x, mod[l, :, 2], w_conv_out[l].astype(BF16),
                        w_attn_out[l].astype(BF16), w_o[l].astype(BF16), g_post[l], nxt)
        x, h = res if nxt is not None else (res, None)
    return x
```

```python
import functools
import math

import jax
import jax.numpy as jnp
import numpy as np
from jax import lax
from jax.experimental import pallas as pl
from jax.experimental.pallas import tpu as pltpu

F32 = jnp.float32
BF16 = jnp.bfloat16

NORM_EPS = 1e-6
CONV_KERNEL = 31
N_HEADS = 8
LOG2E = math.log2(math.e)

LANES = 128
SUBLANES = 8
V7X_VMEM_BYTES = 64 * 1024 * 1024
VMEM_LIMIT = V7X_VMEM_BYTES - 8 * 1024 * 1024

COL_CV, COL_CG, COL_CZ, COL_Q, COL_K, COL_V, COL_AZ, COL_MC, COL_MA = 0, 1, 2, 3, 4, 5, 6, 7, 9


def _lambda_init(layer_idx):
    return 0.8 - 0.6 * math.exp(-0.3 * layer_idx)


def _sigmoid(v):
    return 1.0 / (1.0 + jnp.exp(-v))


def _silu(v):
    return v * _sigmoid(v)


def _ada_kernel(c_ref, w_ref, b_ref, o_ref):
    c = c_ref[...]
    ca = _silu(c).astype(BF16)
    o_ref[0] = jnp.dot(ca, w_ref[0].astype(BF16), preferred_element_type=F32) + b_ref[0]


def _ada_call(c, w_ada, b_ada):
    depth, d, d3 = w_ada.shape
    bsz = c.shape[0]
    bn = 1024
    return pl.pallas_call(
        _ada_kernel,
        grid=(depth, d3 // bn),
        in_specs=[
            pl.BlockSpec((bsz, d), lambda l, n: (0, 0)),
            pl.BlockSpec((1, d, bn), lambda l, n: (l, 0, n)),
            pl.BlockSpec((1, 1, bn), lambda l, n: (l, 0, n)),
        ],
        out_specs=pl.BlockSpec((1, bsz, bn), lambda l, n: (l, 0, n)),
        out_shape=jax.ShapeDtypeStruct((depth, bsz, d3), F32),
        compiler_params=pltpu.CompilerParams(
            dimension_semantics=("arbitrary", "arbitrary"), vmem_limit_bytes=VMEM_LIMIT),
        name="ada_mod",
    )(c, w_ada, b_ada.reshape(depth, 1, d3))


def _modulated_norm(xs, gm, sh):
    ms = jnp.mean(xs * xs, axis=-1, keepdims=True)
    return ((xs * lax.rsqrt(ms + NORM_EPS)) * gm + sh).astype(BF16)


def _norm_kernel(x_ref, g_ref, sc_ref, sh_ref, o_ref, *, bm, rows):
    gm = g_ref[...] * (1.0 + sc_ref[0])
    sh = sh_ref[0]

    def body(r, carry):
        r0 = pl.multiple_of(r * rows, rows)
        o_ref[0, pl.ds(r0, rows), :] = _modulated_norm(x_ref[0, pl.ds(r0, rows), :], gm, sh)
        return carry

    lax.fori_loop(0, bm // rows, body, 0, unroll=4)


def _norm_call(x, g_pre, scale, shift):
    bsz, seq, d = x.shape
    bm = 512
    vec = pl.BlockSpec((1, 1, d), lambda b, m: (b, 0, 0))
    return pl.pallas_call(
        functools.partial(_norm_kernel, bm=bm, rows=16),
        grid=(bsz, seq // bm),
        in_specs=[pl.BlockSpec((1, bm, d), lambda b, m: (b, m, 0)),
                  pl.BlockSpec((1, d), lambda b, m: (0, 0)), vec, vec],
        out_specs=pl.BlockSpec((1, bm, d), lambda b, m: (b, m, 0)),
        out_shape=jax.ShapeDtypeStruct((bsz, seq, d), BF16),
        compiler_params=pltpu.CompilerParams(
            dimension_semantics=("arbitrary", "arbitrary"), vmem_limit_bytes=VMEM_LIMIT),
        name="pre_norm",
    )(x, g_pre.reshape(1, d), scale, shift)


def _in_kernel(h_ref, w_ref, o_ref, kv_ref, *, q_tile, k_tile, q_mult):
    n = pl.program_id(2)
    acc = jnp.dot(h_ref[0], w_ref[0], preferred_element_type=F32)
    mult = jnp.where(n == q_tile, jnp.float32(q_mult), jnp.float32(1.0))
    res = (acc * mult).astype(BF16)
    o_ref[0] = res

    @pl.when((n == k_tile) | (n == k_tile + 1))
    def _():
        for hh in range(kv_ref.shape[2]):
            kv_ref[0, 0, hh] = res[:, hh * LANES:(hh + 1) * LANES]


def _in_call(h, w_in_bf16, layer, *, head_dim):
    bsz, seq, d = h.shape
    in_width = w_in_bf16.shape[2]
    bm, bn = 2048, d // 2
    return pl.pallas_call(
        functools.partial(_in_kernel, q_tile=COL_Q, k_tile=COL_K, q_mult=(head_dim ** -0.5) * LOG2E),
        grid=(bsz, seq // bm, in_width // bn),
        in_specs=[
            pl.BlockSpec((1, bm, d), lambda b, m, n: (b, m, 0)),
            pl.BlockSpec((1, d, bn), lambda b, m, n: (layer, 0, n)),
        ],
        out_specs=(pl.BlockSpec((1, bm, bn), lambda b, m, n: (b, m, n)),
                   pl.BlockSpec((1, 1, bn // LANES, bm, LANES),
                                lambda b, m, n: (b, jnp.clip(n - COL_K, 0, 1), 0, m, 0))),
        out_shape=(jax.ShapeDtypeStruct((bsz, seq, in_width), BF16),
                   jax.ShapeDtypeStruct((bsz, 2, bn // LANES, seq, LANES), BF16)),
        compiler_params=pltpu.CompilerParams(
            dimension_semantics=("arbitrary", "arbitrary", "arbitrary"),
            vmem_limit_bytes=VMEM_LIMIT),
        name="in_proj",
    )(h, w_in_bf16)


ONES_ROWS = 16
N_POS = 9


def _split3_bf16(v):
    p1 = v.astype(BF16).astype(F32)
    p2 = (v - p1).astype(BF16).astype(F32)
    p3 = (v - p1 - p2).astype(BF16).astype(F32)
    return p1, p2, p3


def _attn_kernel(slope_ref, q_ref, k_ref, v_ref, az_ref, lam_ref, g_ref, o_ref,
                 vt_scr, pos_scr, q1_scr, q2_scr, acc1_scr, acc2_scr,
                 sa1_scr, sa2_scr, sb1_scr, sb2_scr, p1_scr, p2_scr, m_scr, *, bq, bk, half, lam0):
    h = pl.program_id(1)
    i = pl.program_id(2)
    slope = slope_ref[h]
    vd = 2 * half
    nkv = vt_scr.shape[0]
    per_q = bq // bk

    @pl.when(i == 0)
    def _():
        for j in range(nkv):
            vt_scr[j, 0:vd, :] = v_ref[j * bk:(j + 1) * bk, :].T
            vt_scr[j, vd:, :] = jnp.ones((ONES_ROWS, bk), BF16)
        ki = lax.broadcasted_iota(jnp.int32, (bk, vd), 0)
        col = lax.broadcasted_iota(jnp.int32, (bk, vd), 1)
        lo = ki & 1
        feat = jnp.where(col < 3, ki - lo, jnp.where(col < 6, lo, jnp.where(col < N_POS, 1, 0)))
        pos_scr[...] = feat.astype(F32).astype(BF16)
        cvec = jnp.full((1, bq), slope, F32)
        c1, c2, c3 = _split3_bf16(cvec)
        qi = lax.broadcasted_iota(jnp.int32, (1, bq), 1).astype(F32)
        r1, r2, r3 = _split3_bf16(-(cvec * qi))
        row = lax.broadcasted_iota(jnp.int32, (vd, bq), 0)
        rows = jnp.zeros((vd, bq), F32)
        for r, val in enumerate((c1, c2, c3, c1, c2, c3, r1, r2, r3)):
            rows = jnp.where(row == r, val, rows)
        q1_scr[vd:, :] = rows.astype(BF16)
        q2_scr[vd:, :] = rows.astype(BF16)
        acc1_scr[...] = jnp.zeros_like(acc1_scr)
        acc2_scr[...] = jnp.zeros_like(acc2_scr)
        p1_scr[...] = jnp.zeros_like(p1_scr)
        p2_scr[...] = jnp.zeros_like(p2_scr)

    qt = q_ref[0].astype(F32).T
    row = lax.broadcasted_iota(jnp.int32, qt.shape, 0)
    q1_scr[0:vd, :] = jnp.where(row < half, qt, 0.0).astype(BF16)
    q2_scr[0:vd, :] = jnp.where(row >= half, qt, 0.0).astype(BF16)

    def scores(j, q_scr, s_ref, lanes=slice(None)):
        k = k_ref[pl.ds(pl.multiple_of(j * bk, bk), bk), :]
        ka = jnp.concatenate([k, pos_scr[...]], axis=1)
        s = jnp.dot(ka, q_scr[:, lanes], preferred_element_type=F32)
        s_ref[:, lanes] = s
        return jnp.max(s, axis=0, keepdims=True)

    def softmax(s, smax, off, m_old):
        m_new = jnp.maximum(m_old, smax + off)
        alpha = jnp.exp2(m_old - m_new)
        p = jnp.exp2(s + (off - m_new)).astype(BF16)
        return m_new, alpha, p

    def flush(j, lanes=slice(None)):
        vt = vt_scr[jnp.maximum(j, 0)]
        acc1_scr[:, lanes] += jnp.dot(vt, p1_scr[:, lanes], preferred_element_type=F32)
        acc2_scr[:, lanes] += jnp.dot(vt, p2_scr[:, lanes], preferred_element_type=F32)

    def step(j, cur, nxt, carry, keep=None, next_lanes=slice(None)):
        m1, m2, smax1, smax2 = carry
        off = slope * (j * bk - i * bq).astype(F32)
        flush(j - 1)
        next1 = scores(j + 1, q1_scr, nxt[0], next_lanes)
        next2 = scores(j + 1, q2_scr, nxt[1], next_lanes)
        s1, s2 = cur[0][...], cur[1][...]
        if keep is not None:
            s1, s2 = jnp.where(keep, s1, -jnp.inf), jnp.where(keep, s2, -jnp.inf)
            smax1, smax2 = jnp.max(s1, axis=0, keepdims=True), jnp.max(s2, axis=0, keepdims=True)
        m1, alpha1, p1 = softmax(s1, smax1, off, m1)
        acc1_scr[...] = alpha1 * acc1_scr[...]
        p1_scr[...] = p1
        m2, alpha2, p2 = softmax(s2, smax2, off, m2)
        acc2_scr[...] = alpha2 * acc2_scr[...]
        p2_scr[...] = p2
        return m1, m2, next1, next2

    assert per_q == 2
    buf_a, buf_b = (sa1_scr, sa2_scr), (sb1_scr, sb2_scr)
    neg = jnp.full((1, bq), -jnp.inf, F32)
    first1 = scores(0, q1_scr, sa1_scr)
    first2 = scores(0, q2_scr, sa2_scr)

    def pair(a, carry):
        carry = step(a, buf_a, buf_b, carry)
        return step(a + 1, buf_b, buf_a, carry)

    carry = lax.fori_loop(0, i // 2, lambda t, c: pair(4 * t + 2, pair(4 * t, c)),
                          (neg, neg, first1, first2))
    carry = lax.cond(i % 2 == 1, lambda c: pair(per_q * (i - 1), c), lambda c: c, carry)

    ki = lax.broadcasted_iota(jnp.int32, (bk, bq), 0)
    qi = lax.broadcasted_iota(jnp.int32, (bk, bq), 1)
    d0 = per_q * i
    hi = slice(bk, bq)
    m1, m2, _, _ = step(d0, buf_a, buf_b, carry, keep=ki <= qi, next_lanes=hi)
    off = slope * ((d0 + 1) * bk - i * bq).astype(F32)
    flush(d0)
    m_scr[0:SUBLANES, :] = jnp.broadcast_to(m1, (SUBLANES, bq))
    m_scr[SUBLANES:, :] = jnp.broadcast_to(m2, (SUBLANES, bq))
    keep_hi = (lax.broadcasted_iota(jnp.int32, (bk, bk), 0)
               <= lax.broadcasted_iota(jnp.int32, (bk, bk), 1))
    for n, (s_ref, acc_scr, p_scr) in enumerate(((sb1_scr, acc1_scr, p1_scr),
                                                 (sb2_scr, acc2_scr, p2_scr))):
        m_old = m_scr[n * SUBLANES:n * SUBLANES + 1, hi]
        s_hi = jnp.where(keep_hi, s_ref[:, hi], -jnp.inf)
        _, alpha, p = softmax(s_hi, jnp.max(s_hi, axis=0, keepdims=True), off, m_old)
        acc_scr[:, hi] = alpha * acc_scr[:, hi]
        p_scr[:, hi] = p
    flush(d0 + 1, hi)

    lq1, lk1, lq2, lk2 = lam_ref[0:1, :], lam_ref[1:2, :], lam_ref[2:3, :], lam_ref[3:4, :]
    lam = (jnp.exp(jnp.sum(lq1 * lk1, axis=-1, keepdims=True))
           - jnp.exp(jnp.sum(lq2 * lk2, axis=-1, keepdims=True)) + lam0)
    inv1 = 1.0 / acc1_scr[vd:vd + 1, :]
    inv2 = 1.0 / acc2_scr[vd:vd + 1, :]
    ot = acc1_scr[0:vd, :] * inv1 - lam * (acc2_scr[0:vd, :] * inv2)
    ms = jnp.mean(ot * ot, axis=0, keepdims=True)
    ot = ot * lax.rsqrt(ms + NORM_EPS) * g_ref[...] * (1.0 - lam0)
    az = az_ref[0].astype(F32)
    o_ref[0] = (ot.T * _silu(az)).astype(BF16)


def _attn_call(p, kv, lam_vecs, subln_g, slopes, *, layer_idx, d):
    bsz, seq, _ = p.shape
    bq, bk = 1024, 512
    nkv = seq // bk
    v_dim = d // (2 * N_HEADS)
    assert v_dim == LANES
    cw = d // 2
    col = lambda c: c * cw // v_dim
    return pl.pallas_call(
        functools.partial(_attn_kernel, bq=bq, bk=bk, half=v_dim // 2,
                          lam0=_lambda_init(layer_idx)),
        grid=(bsz, N_HEADS, seq // bq),
        in_specs=[
            pl.BlockSpec(memory_space=pltpu.SMEM),
            pl.BlockSpec((1, bq, v_dim), lambda b, h, i: (b, i, col(COL_Q) + h)),
            pl.BlockSpec((None, None, None, seq, v_dim), lambda b, h, i: (b, 0, h, 0, 0)),
            pl.BlockSpec((None, None, None, seq, v_dim), lambda b, h, i: (b, 1, h, 0, 0)),
            pl.BlockSpec((1, bq, v_dim), lambda b, h, i: (b, i, col(COL_AZ) + h)),
            pl.BlockSpec((4, v_dim // 2), lambda b, h, i: (0, 0)),
            pl.BlockSpec((v_dim, 1), lambda b, h, i: (0, 0)),
        ],
        out_specs=pl.BlockSpec((1, bq, v_dim), lambda b, h, i: (b, i, h)),
        out_shape=jax.ShapeDtypeStruct((bsz, seq, N_HEADS * v_dim), BF16),
        scratch_shapes=[
            pltpu.VMEM((nkv, v_dim + ONES_ROWS, bk), BF16),
            pltpu.VMEM((bk, v_dim), BF16),
            pltpu.VMEM((2 * v_dim, bq), BF16),
            pltpu.VMEM((2 * v_dim, bq), BF16),
            pltpu.VMEM((v_dim + ONES_ROWS, bq), F32),
            pltpu.VMEM((v_dim + ONES_ROWS, bq), F32),
            pltpu.VMEM((bk, bq), F32), pltpu.VMEM((bk, bq), F32),
            pltpu.VMEM((bk, bq), F32), pltpu.VMEM((bk, bq), F32),
            pltpu.VMEM((bk, bq), BF16), pltpu.VMEM((bk, bq), BF16),
            pltpu.VMEM((2 * SUBLANES, bq), F32),
        ],
        compiler_params=pltpu.CompilerParams(
            dimension_semantics=("arbitrary", "arbitrary", "arbitrary"),
            vmem_limit_bytes=VMEM_LIMIT),
        name="diff_attn",
    )(slopes, p, kv, kv, p, lam_vecs, subln_g.reshape(v_dim, 1))


def _conv_kernel(cv_ref, cg_ref, cvh_ref, cgh_ref, cz_ref, w_ref, b_ref, g_ref, be_ref, o_ref,
                 u_scr, sh_scr, *, bm, halo, rows):
    i = pl.program_id(1)
    uh = cvh_ref[0].astype(F32) * _sigmoid(cgh_ref[0].astype(F32))
    u_scr[0:halo, :] = jnp.where(i > 0, uh, 0.0)
    u_scr[halo:halo + bm, :] = cv_ref[0].astype(F32) * _sigmoid(cg_ref[0].astype(F32))
    n_sh = sh_scr.shape[1]
    u_all = u_scr[...]
    for b in range(1, SUBLANES):
        sh_scr[b - 1] = pltpu.roll(u_all, halo + bm - b, 0)[0:n_sh, :]
    first = halo - (CONV_KERNEL - 1)
    cw = u_scr.shape[1]
    for r0 in range(0, bm, rows):
        acc = jnp.zeros((rows // SUBLANES, SUBLANES, cw), F32)
        for t in range(CONV_KERNEL):
            b = (first + t) % SUBLANES
            a0 = r0 + first + t - b
            tap = u_scr[a0:a0 + rows, :] if b == 0 else sh_scr[b - 1, a0:a0 + rows, :]
            acc = acc + tap.reshape(acc.shape) * w_ref[t]
        acc = acc.reshape(rows, cw) + b_ref[...]
        mu = jnp.mean(acc, axis=-1, keepdims=True)
        cen = acc - mu
        var = jnp.mean(cen * cen, axis=-1, keepdims=True)
        y = cen * lax.rsqrt(var + NORM_EPS) * g_ref[...] + be_ref[...]
        cz = cz_ref[0, r0:r0 + rows, :].astype(F32)
        o_ref[0, r0:r0 + rows, :] = (_silu(y) * _silu(cz)).astype(BF16)


def _conv_call(p, conv_w, conv_b, cn_g, cn_b):
    bsz, seq, _ = p.shape
    cw = conv_w.shape[1]
    bm, halo = 256, 32
    assert halo >= CONV_KERNEL - 1 and bm % halo == 0
    hb = bm // halo
    cur = lambda c: pl.BlockSpec((1, bm, cw), lambda b, i: (b, i, c))
    prev = lambda c: pl.BlockSpec((1, halo, cw), lambda b, i: (b, jnp.maximum(i * hb - 1, 0), c))
    vec = pl.BlockSpec((1, cw), lambda b, i: (0, 0))
    return pl.pallas_call(
        functools.partial(_conv_kernel, bm=bm, halo=halo, rows=32),
        grid=(bsz, seq // bm),
        in_specs=[cur(COL_CV), cur(COL_CG), prev(COL_CV), prev(COL_CG), cur(COL_CZ),
                  pl.BlockSpec((CONV_KERNEL, SUBLANES, cw), lambda b, i: (0, 0, 0)), vec, vec, vec],
        out_specs=pl.BlockSpec((1, bm, cw), lambda b, i: (b, i, 0)),
        out_shape=jax.ShapeDtypeStruct((bsz, seq, cw), BF16),
        scratch_shapes=[pltpu.VMEM((halo + bm, cw), F32),
                        pltpu.VMEM((SUBLANES - 1, halo + bm - SUBLANES, cw), F32)],
        compiler_params=pltpu.CompilerParams(
            dimension_semantics=("arbitrary", "arbitrary"), vmem_limit_bytes=VMEM_LIMIT),
        name="conv_branch",
    )(p, p, p, p, p, jnp.broadcast_to(conv_w[:, None, :], (CONV_KERNEL, SUBLANES, cw)),
      conv_b.reshape(1, cw), cn_g.reshape(1, cw), cn_b.reshape(1, cw))


def _out_kernel(ac_ref, aa_ref, mc0_ref, mc1_ref, ma0_ref, ma1_ref, x_ref, gate_ref,
                wc_ref, wa_ref, wo_ref, g_ref, *rest, emit_h):
    yc = jnp.dot(ac_ref[0], wc_ref[0], preferred_element_type=F32)
    ya = jnp.dot(aa_ref[0], wa_ref[0], preferred_element_type=F32)
    mc = jnp.concatenate([mc0_ref[0], mc1_ref[0]], axis=-1).astype(F32)
    ma = jnp.concatenate([ma0_ref[0], ma1_ref[0]], axis=-1).astype(F32)
    merged = _sigmoid(mc) * yc + _sigmoid(ma) * ya
    out = jnp.dot(merged.astype(BF16), wo_ref[0], preferred_element_type=F32)
    ms = jnp.mean(out * out, axis=-1, keepdims=True)
    y = out * lax.rsqrt(ms + NORM_EPS) * g_ref[...]
    x_new = x_ref[0] + gate_ref[0] * y
    if emit_h:
        gn_ref, scn_ref, shn_ref, o_ref, h_ref = rest
        h_ref[0] = _modulated_norm(x_new, gn_ref[...] * (1.0 + scn_ref[0]), shn_ref[0])
    else:
        (o_ref,) = rest
    o_ref[0] = x_new


def _out_call(p, a_conv, a_attn, x, gate, wc, wa, wo, layer, g_post, next_norm=None):
    bsz, seq, d = x.shape
    cw = d // 2
    bm = 256
    resident = lambda shape: pl.BlockSpec((1,) + shape, lambda b, i: (layer, 0, 0),
                                          pipeline_mode=pl.Buffered(1))
    seg = lambda c: pl.BlockSpec((1, bm, cw), lambda b, i: (b, i, c))
    rows = pl.BlockSpec((1, bm, d), lambda b, i: (b, i, 0))
    per_batch = pl.BlockSpec((1, 1, d), lambda b, i: (b, 0, 0))
    shared = pl.BlockSpec((1, d), lambda b, i: (0, 0))
    in_specs = [
        pl.BlockSpec((1, bm, cw), lambda b, i: (b, i, 0)),
        pl.BlockSpec((1, bm, cw), lambda b, i: (b, i, 0)),
        seg(COL_MC), seg(COL_MC + 1), seg(COL_MA), seg(COL_MA + 1),
        rows, per_batch,
        resident((cw, d)), resident((cw, d)), resident((d, d)),
        shared,
    ]
    args = [a_conv, a_attn, p, p, p, p, x, gate, wc, wa, wo, g_post.reshape(1, d)]
    out_specs, out_shape = rows, jax.ShapeDtypeStruct((bsz, seq, d), F32)
    if next_norm is not None:
        g_next, scale_next, shift_next = next_norm
        in_specs += [shared, per_batch, per_batch]
        args += [g_next.reshape(1, d), scale_next, shift_next]
        out_specs = (rows, rows)
        out_shape = (out_shape, jax.ShapeDtypeStruct((bsz, seq, d), BF16))
    return pl.pallas_call(
        functools.partial(_out_kernel, emit_h=next_norm is not None),
        grid=(bsz, seq // bm),
        in_specs=in_specs,
        out_specs=out_specs,
        out_shape=out_shape,
        compiler_params=pltpu.CompilerParams(
            dimension_semantics=("arbitrary", "arbitrary"), vmem_limit_bytes=VMEM_LIMIT),
        name="merge_out",
    )(*args)


def kernel(x, c, w_ada, b_ada, g_pre, g_post, w_in, conv_w, conv_b, cn_g, cn_b, w_conv_out,
           lam_q1, lam_k1, lam_q2, lam_k2, subln_g, w_attn_out, w_o):
    bsz, seq, d = x.shape
    depth = w_ada.shape[0]
    head_dim = lam_q1.shape[1]
    mod = _ada_call(c, w_ada, b_ada).reshape(depth, bsz, 3, 1, d)
    slopes = jnp.asarray(
        (2.0 ** (-8.0 * np.arange(1, N_HEADS + 1) / N_HEADS)) * LOG2E, dtype=F32)
    h = _norm_call(x, g_pre[0], mod[0, :, 1], mod[0, :, 0])
    w_in, w_conv_out, w_attn_out, w_o = (w.astype(BF16) for w in (w_in, w_conv_out, w_attn_out, w_o))
    for l in range(depth):
        p, kv = _in_call(h, w_in, l, head_dim=head_dim)
        lam_vecs = jnp.stack([lam_q1[l], lam_k1[l], lam_q2[l], lam_k2[l]])
        a_attn = _attn_call(p, kv, lam_vecs, subln_g[l], slopes, layer_idx=l, d=d)
        a_conv = _conv_call(p, conv_w[l], conv_b[l], cn_g[l], cn_b[l])
        nxt = (g_pre[l + 1], mod[l + 1, :, 1], mod[l + 1, :, 0]) if l + 1 < depth else None
        res = _out_call(p, a_conv, a_attn, x, mod[l, :, 2], w_conv_out, w_attn_out, w_o, l,
                        g_post[l], nxt)
        x, h = res if nxt is not None else (res, None)
    return x
```
